```python
import math
import jax, jax.numpy as jnp
from jax import lax
import numpy as np

D_MODEL = 1024
BATCH = 2
SEQ = 8192
DEPTH = 4

N_MIXERS = 4
ROPE_THETA = 500000.0
Q_BLOCK = 128
LN_EPS = 1e-5
RMS_EPS = 1e-6
MAX_POS_OFFSET = 4096
DIFF_HEADS = 8
DIFF_HEAD_DIM = 64
FOX_HEADS = 16
FOX_HEAD_DIM = 64
MLA_HEADS = 16
MLA_NOPE = 64
MLA_ROPE = 32
MLA_V = 64
MLA_Q_RANK = 384
MLA_KV_RANK = 256
MOBA_HEADS = 16
MOBA_HEAD_DIM = 64
MOBA_BLOCK = 256
MOBA_TOPK = 3
MOBA_Q_CHUNK = 32
PARTIAL_ROT = 64 // 4
D_FF = 2816
CONV_WIDTH = 3
DEEPNORM_ALPHA = (2 * DEPTH) ** 0.25
DEEPNORM_BETA = (8 * DEPTH) ** -0.25

kernel_name = 'hybrid_interleaved_diff_fox_mla_moba_convffn'


def _n_uses(m):
    return len(range(m, DEPTH, N_MIXERS))


def diff_lambda_init(layer):
    return 0.8 - 0.6 * math.exp(-0.3 * layer)


def layer_norm(x, g, b):
    xf = x.astype(jnp.float32)
    mu = xf.mean(-1, keepdims=True)
    var = jnp.square(xf - mu).mean(-1, keepdims=True)
    y = (xf - mu) * lax.rsqrt(var + LN_EPS) * g.astype(jnp.float32) + b.astype(jnp.float32)
    return y.astype(x.dtype)


def rms_norm(x, g):
    xf = x.astype(jnp.float32)
    y = xf * lax.rsqrt(jnp.mean(xf * xf, -1, keepdims=True) + RMS_EPS) * g.astype(jnp.float32)
    return y.astype(x.dtype)


def rotary_angles(positions, rot_dim):
    inv_freq = ROPE_THETA ** (-jnp.arange(0, rot_dim, 2, dtype=jnp.float32) / rot_dim)
    ang = positions.astype(jnp.float32)[..., None] * inv_freq
    return jnp.cos(ang), jnp.sin(ang)


def apply_rotary(x, cos, sin, rot_dim):
    expand = (1,) * (x.ndim - 3)
    cos = cos.reshape(cos.shape[:2] + expand + cos.shape[-1:])
    sin = sin.reshape(sin.shape[:2] + expand + sin.shape[-1:])
    half = rot_dim // 2
    xr = x[..., :rot_dim].astype(jnp.float32)
    x1, x2 = xr[..., :half], xr[..., half:]
    rot = jnp.concatenate([x1 * cos - x2 * sin, x2 * cos + x1 * sin], -1).astype(x.dtype)
    return jnp.concatenate([rot, x[..., rot_dim:]], -1)


def causal_mask(q_start, q_len, k_len):
    q_idx = q_start + jnp.arange(q_len)
    k_idx = jnp.arange(k_len)
    return k_idx[None, :] <= q_idx[:, None]


def masked_softmax(scores, mask):
    s = jnp.where(mask, scores.astype(jnp.float32), -jnp.inf)
    return jax.nn.softmax(s, axis=-1)


def sweep_query_blocks(block_fn, seq, block):
    starts = jnp.arange(seq // block) * block
    out = lax.map(block_fn, starts)
    n, b, blk, f = out.shape
    return out.transpose(1, 0, 2, 3).reshape(b, n * blk, f)


def diff_attention(x, cos, sin, w_qkv, lam_q1, lam_k1, lam_q2, lam_k2, subln_g, w_o, lambda_init):
    B, S, _ = x.shape
    H, d = DIFF_HEADS, DIFF_HEAD_DIM
    q, k, v = jnp.split(x @ w_qkv, 3, axis=-1)
    q = apply_rotary(q.reshape(B, S, H, 2, d), cos, sin, PARTIAL_ROT).transpose(0, 2, 3, 1, 4)
    k = apply_rotary(k.reshape(B, S, H, 2, d), cos, sin, PARTIAL_ROT).transpose(0, 2, 3, 1, 4)
    v = v.reshape(B, S, H, 2 * d).transpose(0, 2, 1, 3)
    f32 = jnp.float32
    lam = (jnp.exp(jnp.sum(lam_q1.astype(f32) * lam_k1.astype(f32)))
           - jnp.exp(jnp.sum(lam_q2.astype(f32) * lam_k2.astype(f32))) + lambda_init)
    scale = d ** -0.5

    def block(start):
        qb = lax.dynamic_slice_in_dim(q, start, Q_BLOCK, axis=3)
        s = jnp.einsum('bhcqd,bhckd->bhcqk', qb, k) * scale
        p = masked_softmax(s, causal_mask(start, Q_BLOCK, S))
        a = (p[:, :, 0] - lam * p[:, :, 1]).astype(v.dtype)
        o = jnp.einsum('bhqk,bhkd->bqhd', a, v)
        o = rms_norm(o, subln_g) * (1.0 - lambda_init)
        return o.reshape(B, Q_BLOCK, H * 2 * d)

    return sweep_query_blocks(block, S, Q_BLOCK) @ w_o


def forgetting_attention(x, w_in, b_f, w_o):
    B, S, _ = x.shape
    H, d = FOX_HEADS, FOX_HEAD_DIM
    q, k, v, f_logit = jnp.split(x @ w_in, [H * d, 2 * H * d, 3 * H * d], axis=-1)
    q = q.reshape(B, S, H, d).transpose(0, 2, 1, 3)
    k = k.reshape(B, S, H, d).transpose(0, 2, 1, 3)
    v = v.reshape(B, S, H, d).transpose(0, 2, 1, 3)
    log_f = jax.nn.log_sigmoid((f_logit + b_f).astype(jnp.float32))
    c = jnp.cumsum(log_f, axis=1).transpose(0, 2, 1)
    scale = d ** -0.5

    def block(start):
        qb = lax.dynamic_slice_in_dim(q, start, Q_BLOCK, axis=2)
        cb = lax.dynamic_slice_in_dim(c, start, Q_BLOCK, axis=2)
        s = (jnp.einsum('bhqd,bhkd->bhqk', qb, k).astype(jnp.float32) * scale
             + cb[..., :, None] - c[..., None, :])
        p = masked_softmax(s, causal_mask(start, Q_BLOCK, S)).astype(v.dtype)
        o = jnp.einsum('bhqk,bhkd->bqhd', p, v)
        return o.reshape(B, Q_BLOCK, H * d)

    return sweep_query_blocks(block, S, Q_BLOCK) @ w_o


def latent_attention(x, cos, sin, w_down, q_norm_g, kv_norm_g, w_uq, w_ukv, w_o):
    B, S, _ = x.shape
    H = MLA_HEADS
    c_q, c_kv, k_rope = jnp.split(x @ w_down, [MLA_Q_RANK, MLA_Q_RANK + MLA_KV_RANK], axis=-1)
    c_q = rms_norm(c_q, q_norm_g)
    c_kv = rms_norm(c_kv, kv_norm_g)
    q = (c_q @ w_uq).reshape(B, S, H, MLA_NOPE + MLA_ROPE)
    q_nope = q[..., :MLA_NOPE].transpose(0, 2, 1, 3)
    q_rope = apply_rotary(q[..., MLA_NOPE:], cos, sin, MLA_ROPE).transpose(0, 2, 1, 3)
    kv = (c_kv @ w_ukv).reshape(B, S, H, MLA_NOPE + MLA_V)
    k_nope = kv[..., :MLA_NOPE].transpose(0, 2, 1, 3)
    v = kv[..., MLA_NOPE:].transpose(0, 2, 1, 3)
    k_rope = apply_rotary(k_rope, cos, sin, MLA_ROPE)
    scale = (MLA_NOPE + MLA_ROPE) ** -0.5

    def block(start):
        qn = lax.dynamic_slice_in_dim(q_nope, start, Q_BLOCK, axis=2)
        qr = lax.dynamic_slice_in_dim(q_rope, start, Q_BLOCK, axis=2)
        s = (jnp.einsum('bhqd,bhkd->bhqk', qn, k_nope)
             + jnp.einsum('bhqr,bkr->bhqk', qr, k_rope)) * scale
        p = masked_softmax(s, causal_mask(start, Q_BLOCK, S)).astype(v.dtype)
        o = jnp.einsum('bhqk,bhkd->bqhd', p, v)
        return o.reshape(B, Q_BLOCK, H * MLA_V)

    return sweep_query_blocks(block, S, Q_BLOCK) @ w_o


def moba_attention(x, cos, sin, w_qkv, w_o):
    B, S, _ = x.shape
    H, d, BS, QC = MOBA_HEADS, MOBA_HEAD_DIM, MOBA_BLOCK, MOBA_Q_CHUNK
    q, k, v = jnp.split(x @ w_qkv, 3, axis=-1)
    q = apply_rotary(q.reshape(B, S, H, d), cos, sin, PARTIAL_ROT).transpose(0, 2, 1, 3)
    k = apply_rotary(k.reshape(B, S, H, d), cos, sin, PARTIAL_ROT).transpose(0, 2, 1, 3)
    v = v.reshape(B, S, H, d).transpose(0, 2, 1, 3)
    nb = -(-S // BS)
    pad = nb * BS - S
    k_blocks = jnp.pad(k, ((0, 0), (0, 0), (0, pad), (0, 0))).reshape(B, H, nb, BS, d)
    v_blocks = jnp.pad(v, ((0, 0), (0, 0), (0, pad), (0, 0))).reshape(B, H, nb, BS, d)
    k_mean = k_blocks.astype(jnp.float32).mean(axis=3).astype(k.dtype)
    topk = min(MOBA_TOPK, max(nb - 1, 1))
    bidx = jnp.arange(B)[:, None, None, None]
    hidx = jnp.arange(H)[None, :, None, None]
    scale = d ** -0.5

    def chunk(start):
        qc = lax.dynamic_slice_in_dim(q, start, QC, axis=2)
        cur = start // BS
        gate = jnp.einsum('bhqd,bhnd->bhqn', qc, k_mean).astype(jnp.float32)
        gate = jnp.where(jnp.arange(nb) < cur, gate, -jnp.inf)
        _, idx = lax.top_k(gate, topk)
        sel_ok = idx < cur
        k_sel = k_blocks[bidx, hidx, idx]
        v_sel = v_blocks[bidx, hidx, idx]
        s_sel = jnp.einsum('bhqd,bhqnkd->bhqnk', qc, k_sel).reshape(B, H, QC, topk * BS)
        k_own = lax.dynamic_index_in_dim(k_blocks, cur, axis=2, keepdims=False)
        v_own = lax.dynamic_index_in_dim(v_blocks, cur, axis=2, keepdims=False)
        s_own = jnp.einsum('bhqd,bhkd->bhqk', qc, k_own)
        own_mask = (cur * BS + jnp.arange(BS))[None, :] <= (start + jnp.arange(QC))[:, None]
        s = jnp.concatenate([s_sel, s_own], -1) * scale
        mask = jnp.concatenate([jnp.repeat(sel_ok, BS, axis=-1),
                                jnp.broadcast_to(own_mask, (B, H, QC, BS))], -1)
        p = masked_softmax(s, mask).astype(v.dtype)
        p_sel = p[..., :topk * BS].reshape(B, H, QC, topk, BS)
        o = (jnp.einsum('bhqnk,bhqnkd->bqhd', p_sel, v_sel)
             + jnp.einsum('bhqk,bhkd->bqhd', p[..., topk * BS:], v_own))
        return o.reshape(B, QC, H * d)

    return sweep_query_blocks(chunk, S, QC) @ w_o


def conv_ffn(x, w_in, conv_w, conv_b, w_out):
    gate, up = jnp.split(x @ w_in, 2, axis=-1)
    gate = lax.conv_general_dilated(
        gate, conv_w[:, None, :], window_strides=(1,), padding=[(CONV_WIDTH - 1, 0)],
        dimension_numbers=('NWC', 'WIO', 'NWC'), feature_group_count=D_FF) + conv_b
    return (jax.nn.silu(gate) * up) @ w_out


def setup_inputs(seed: int = 0) -> dict:
    key = jax.random.key(seed)
    keys = iter(jax.random.split(key, 32))
    f32 = jnp.float32

    def normal(shape, std):
        return std * jax.random.normal(next(keys), shape, f32)

    def gain(shape):
        return 1.0 + normal(shape, 0.02)

    n_diff, n_fox, n_mla, n_moba = [_n_uses(m) for m in range(N_MIXERS)]
    D = D_MODEL
    beta = DEEPNORM_BETA
    x = normal((BATCH, SEQ, D), 1.0)
    offset = jax.random.randint(next(keys), (BATCH, 1), 0, MAX_POS_OFFSET, dtype=jnp.int32)
    positions = jnp.arange(SEQ, dtype=jnp.int32)[None, :] + offset
    diff_w = DIFF_HEADS * 2 * DIFF_HEAD_DIM
    fox_w = FOX_HEADS * FOX_HEAD_DIM
    moba_w = MOBA_HEADS * MOBA_HEAD_DIM
    return {
        'x': x,
        'positions': positions,
        'diff_w_qkv': normal((n_diff, D, 3 * diff_w), D ** -0.5),
        'diff_lambda_q1': normal((n_diff, DIFF_HEAD_DIM), 0.1),
        'diff_lambda_k1': normal((n_diff, DIFF_HEAD_DIM), 0.1),
        'diff_lambda_q2': normal((n_diff, DIFF_HEAD_DIM), 0.1),
        'diff_lambda_k2': normal((n_diff, DIFF_HEAD_DIM), 0.1),
        'diff_subln_g': gain((n_diff, 2 * DIFF_HEAD_DIM)),
        'diff_w_o': normal((n_diff, diff_w, D), beta * diff_w ** -0.5),
        'fox_w_in': normal((n_fox, D, 3 * fox_w + FOX_HEADS), D ** -0.5),
        'fox_b_f': jax.random.uniform(next(keys), (n_fox, FOX_HEADS), f32, 1.0, 5.0),
        'fox_w_o': normal((n_fox, fox_w, D), beta * fox_w ** -0.5),
        'mla_w_down': normal((n_mla, D, MLA_Q_RANK + MLA_KV_RANK + MLA_ROPE), D ** -0.5),
        'mla_q_norm_g': gain((n_mla, MLA_Q_RANK)),
        'mla_kv_norm_g': gain((n_mla, MLA_KV_RANK)),
        'mla_w_uq': normal((n_mla, MLA_Q_RANK, MLA_HEADS * (MLA_NOPE + MLA_ROPE)), MLA_Q_RANK ** -0.5),
        'mla_w_ukv': normal((n_mla, MLA_KV_RANK, MLA_HEADS * (MLA_NOPE + MLA_V)), MLA_KV_RANK ** -0.5),
        'mla_w_o': normal((n_mla, MLA_HEADS * MLA_V, D), beta * (MLA_HEADS * MLA_V) ** -0.5),
        'moba_w_qkv': normal((n_moba, D, 3 * moba_w), D ** -0.5),
        'moba_w_o': normal((n_moba, moba_w, D), beta * moba_w ** -0.5),
        'ffn_w_in': normal((DEPTH, D, 2 * D_FF), D ** -0.5),
        'ffn_conv_w': normal((DEPTH, CONV_WIDTH, D_FF), CONV_WIDTH ** -0.5),
        'ffn_conv_b': normal((DEPTH, D_FF), 0.02),
        'ffn_w_out': normal((DEPTH, D_FF, D), beta * D_FF ** -0.5),
        'ln1_g': gain((DEPTH, D)),
        'ln1_b': normal((DEPTH, D), 0.02),
        'ln2_g': gain((DEPTH, D)),
        'ln2_b': normal((DEPTH, D), 0.02),
    }


def reference(x, positions, diff_w_qkv, diff_lambda_q1, diff_lambda_k1, diff_lambda_q2, diff_lambda_k2,
              diff_subln_g, diff_w_o, fox_w_in, fox_b_f, fox_w_o, mla_w_down, mla_q_norm_g, mla_kv_norm_g,
              mla_w_uq, mla_w_ukv, mla_w_o, moba_w_qkv, moba_w_o, ffn_w_in, ffn_conv_w, ffn_conv_b,
              ffn_w_out, ln1_g, ln1_b, ln2_g, ln2_b):
    cos_p, sin_p = rotary_angles(positions, PARTIAL_ROT)
    cos_m, sin_m = rotary_angles(positions, MLA_ROPE)
    h = x
    for i in range(DEPTH):
        m, u = i % N_MIXERS, i // N_MIXERS
        if m == 0:
            y = diff_attention(h, cos_p, sin_p, diff_w_qkv[u], diff_lambda_q1[u], diff_lambda_k1[u],
                               diff_lambda_q2[u], diff_lambda_k2[u], diff_subln_g[u], diff_w_o[u],
                               diff_lambda_init(i))
        elif m == 1:
            y = forgetting_attention(h, fox_w_in[u], fox_b_f[u], fox_w_o[u])
        elif m == 2:
            y = latent_attention(h, cos_m, sin_m, mla_w_down[u], mla_q_norm_g[u], mla_kv_norm_g[u],
                                 mla_w_uq[u], mla_w_ukv[u], mla_w_o[u])
        else:
            y = moba_attention(h, cos_p, sin_p, moba_w_qkv[u], moba_w_o[u])
        h = layer_norm(DEEPNORM_ALPHA * h + y, ln1_g[i], ln1_b[i])
        f = conv_ffn(h, ffn_w_in[i], ffn_conv_w[i], ffn_conv_b[i], ffn_w_out[i])
        h = layer_norm(DEEPNORM_ALPHA * h + f, ln2_g[i], ln2_b[i])
    return h
```

```python
import functools
import math

import jax
import jax.numpy as jnp
import numpy as np
from jax import lax
from jax.experimental import pallas as pl
from jax.experimental.pallas import tpu as pltpu

F32 = jnp.float32
BF16 = jnp.bfloat16

LANES = 128
VMEM_LIMIT = 56 * 1024 * 1024

ROPE_THETA = 500000.0
LN_EPS = 1e-5
RMS_EPS = 1e-6
HEAD_DIM = 64
PARTIAL_ROT = HEAD_DIM // 4
DIFF_HEADS = 8
FOX_HEADS = 16
MLA_HEADS = 16
MLA_ROPE = 32
MLA_Q_RANK = 384
MLA_KV_RANK = 256
MOBA_HEADS = 16
MOBA_BLOCK = 256
MOBA_TOPK = 3
CONV_WIDTH = 3
N_MIXERS = 4

NEG = -1e30
LOG2E = math.log2(math.e)


def _params(*sem):
    return pltpu.CompilerParams(dimension_semantics=sem, vmem_limit_bytes=VMEM_LIMIT)


def _rot_tables(positions, rot_dim, period):
    half = rot_dim // 2
    inv_freq = ROPE_THETA ** (-jnp.arange(0, rot_dim, 2, dtype=F32) / rot_dim)
    ang = positions.astype(F32)[..., None] * inv_freq
    cos, sin = jnp.cos(ang), jnp.sin(ang)
    lane = np.arange(LANES)
    j = lane % period
    idx = j % half
    lo = j < half
    hi = (j >= half) & (j < rot_dim)
    cos_t = jnp.where(lo | hi, cos[..., idx], 1.0)
    sin_a = jnp.where(lo, -sin[..., idx], 0.0)
    sin_b = jnp.where(hi, sin[..., idx], 0.0)
    rows = positions.shape[0] * positions.shape[1]
    return tuple(t.reshape(rows, LANES) for t in (cos_t, sin_a, sin_b)), half


def _apply_rot(val, cos, sin_a, sin_b, half):
    pieces = []
    for c in range(val.shape[1] // LANES):
        blk = val[:, c * LANES:(c + 1) * LANES]
        pieces.append(blk * cos + pltpu.roll(blk, LANES - half, 1) * sin_a + pltpu.roll(blk, half, 1) * sin_b)
    return pieces[0] if len(pieces) == 1 else jnp.concatenate(pieces, axis=-1)


def _proj_kernel(*refs, rot_half, rot_tile0, want_kmean, tm):
    x_ref, w_ref = refs[:2]
    rest = refs[2:]
    if rot_half is not None:
        cos_ref, sa_ref, sb_ref = rest[:3]
        rest = rest[3:]
    o_ref = rest[0]
    km_ref = rest[1] if want_kmean else None
    acc = jnp.dot(x_ref[...], w_ref[...], preferred_element_type=F32)

    def emit(val):
        o_ref[...] = val.astype(o_ref.dtype)
        if want_kmean:
            tn = val.shape[1]
            km_ref[...] = jnp.mean(val.reshape(tm // MOBA_BLOCK, MOBA_BLOCK, tn), axis=1)

    if rot_half is None:
        emit(acc)
        return

    def rotated():
        return _apply_rot(acc, cos_ref[...], sa_ref[...], sb_ref[...], rot_half)

    if rot_tile0 == 0:
        emit(rotated())
    else:
        j = pl.program_id(1)

        @pl.when(j >= rot_tile0)
        def _():
            emit(rotated())

        @pl.when(j < rot_tile0)
        def _():
            emit(acc)


def _proj(x, w, *, out_dtype=BF16, rot=None, rot_from=0, want_kmean=False, tm=1024, tn=512):
    m, k = x.shape
    n = w.shape[1]
    tm, tn = min(tm, m), min(tn, n)
    assert m % tm == 0 and n % tn == 0 and rot_from % tn == 0
    in_specs = [pl.BlockSpec((tm, k), lambda i, j: (i, 0)), pl.BlockSpec((k, tn), lambda i, j: (0, j))]
    args = [x, w]
    rot_half = None
    if rot is not None:
        tabs, rot_half = rot
        in_specs += [pl.BlockSpec((tm, LANES), lambda i, j: (i, 0))] * 3
        args += list(tabs)
    out_shape = [jax.ShapeDtypeStruct((m, n), out_dtype)]
    out_specs = [pl.BlockSpec((tm, tn), lambda i, j: (i, j))]
    if want_kmean:
        assert tm % MOBA_BLOCK == 0
        out_shape.append(jax.ShapeDtypeStruct((m // tm, tm // MOBA_BLOCK, n), F32))
        out_specs.append(pl.BlockSpec((None, tm // MOBA_BLOCK, tn), lambda i, j: (i, 0, j)))
    res = pl.pallas_call(
        functools.partial(_proj_kernel, rot_half=rot_half, rot_tile0=rot_from // tn, want_kmean=want_kmean, tm=tm),
        grid=(m // tm, n // tn),
        in_specs=in_specs,
        out_specs=out_specs,
        out_shape=out_shape,
        compiler_params=_params("parallel", "arbitrary"),
        name="proj",
    )(*args)
    return res if want_kmean else res[0]


def _layer_norm_rows(z, g, b):
    mu = jnp.mean(z, axis=-1, keepdims=True)
    zc = z - mu
    var = jnp.mean(zc * zc, axis=-1, keepdims=True)
    return zc * lax.rsqrt(var + LN_EPS) * g + b


def _proj_ln_kernel(x_ref, w_ref, res_ref, g_ref, b_ref, h32_ref, h16_ref, *, alpha):
    y = jnp.dot(x_ref[...], w_ref[...], preferred_element_type=F32)
    h = _layer_norm_rows(alpha * res_ref[...] + y, g_ref[...], b_ref[...])
    h32_ref[...] = h
    h16_ref[...] = h.astype(BF16)


def _proj_ln(x, w, res, g, b, *, alpha, tm=512):
    m, k = x.shape
    n = w.shape[1]
    tm = min(tm, m)
    row = lambda i: (i, 0)
    const = lambda i: (0, 0)
    return pl.pallas_call(
        functools.partial(_proj_ln_kernel, alpha=alpha),
        grid=(m // tm,),
        in_specs=[pl.BlockSpec((tm, k), row), pl.BlockSpec((k, n), const), pl.BlockSpec((tm, n), row),
                  pl.BlockSpec((1, n), const), pl.BlockSpec((1, n), const)],
        out_specs=[pl.BlockSpec((tm, n), row), pl.BlockSpec((tm, n), row)],
        out_shape=[jax.ShapeDtypeStruct((m, n), F32), jax.ShapeDtypeStruct((m, n), BF16)],
        compiler_params=_params("parallel"),
        name="proj_ln",
    )(x, w, res, g.reshape(1, n), b.reshape(1, n))


def _ffn_out_kernel(gate_ref, up_ref, prev_ref, cw_ref, cb_ref, w_ref, res_ref, g_ref, b_ref, h32_ref, h16_ref,
                    *, alpha, tm, seq, chunk):
    i = pl.program_id(0)
    at_seq_start = (i * tm) % seq == 0
    d_ff = gate_ref.shape[1]
    row = lax.broadcasted_iota(jnp.int32, (tm, 1), 0)
    acc = jnp.zeros((tm, w_ref.shape[1]), F32)
    for c in range(d_ff // chunk):
        cs = slice(c * chunk, (c + 1) * chunk)
        gt = gate_ref[:, cs]
        prev = jnp.where(at_seq_start, 0.0, prev_ref[:, cs])
        g1 = jnp.where(row == 0, prev[7:8, :], pltpu.roll(gt, 1, 0))
        g2 = jnp.where(row == 0, prev[6:7, :], jnp.where(row == 1, prev[7:8, :], pltpu.roll(gt, 2, 0)))
        cw = cw_ref[:, cs]
        conv = cw[0:1, :] * g2 + cw[1:2, :] * g1 + cw[2:3, :] * gt + cb_ref[:, cs]
        act = conv * (1.0 / (1.0 + jnp.exp(-conv))) * up_ref[:, cs]
        acc = acc + jnp.dot(act.astype(BF16), w_ref[cs, :], preferred_element_type=F32)
    h = _layer_norm_rows(alpha * res_ref[...] + acc, g_ref[...], b_ref[...])
    h32_ref[...] = h
    h16_ref[...] = h.astype(BF16)


def _ffn_out(gu, conv_w, conv_b, w_out, res, g, b, *, alpha, seq, tm=256, chunk=256):
    m = gu.shape[0]
    d_ff, n = w_out.shape
    tm = min(tm, m)
    assert seq % tm == 0 and tm % 8 == 0 and d_ff % chunk == 0
    row = lambda i: (i, 0)
    const = lambda i: (0, 0)
    sub = tm // 8
    return pl.pallas_call(
        functools.partial(_ffn_out_kernel, alpha=alpha, tm=tm, seq=seq, chunk=chunk),
        grid=(m // tm,),
        in_specs=[pl.BlockSpec((tm, d_ff), lambda i: (i, 0)),
                  pl.BlockSpec((tm, d_ff), lambda i: (i, 1)),
                  pl.BlockSpec((8, d_ff), lambda i: (jnp.maximum(i * sub - 1, 0), 0)),
                  pl.BlockSpec((CONV_WIDTH, d_ff), const), pl.BlockSpec((1, d_ff), const),
                  pl.BlockSpec((d_ff, n), const), pl.BlockSpec((tm, n), row),
                  pl.BlockSpec((1, n), const), pl.BlockSpec((1, n), const)],
        out_specs=[pl.BlockSpec((tm, n), row), pl.BlockSpec((tm, n), row)],
        out_shape=[jax.ShapeDtypeStruct((m, n), F32), jax.ShapeDtypeStruct((m, n), BF16)],
        compiler_params=_params("parallel"),
        name="ffn_out",
    )(gu, gu, gu, conv_w, conv_b.reshape(1, d_ff), w_out, res, g.reshape(1, n), b.reshape(1, n))


def _split3_bf16(v):
    hi = v.astype(BF16)
    r1 = v - hi.astype(F32)
    mid = r1.astype(BF16)
    lo = (r1 - mid.astype(F32)).astype(BF16)
    return hi, mid, lo


def _fox_gate_kernel(x_ref, w_ref, bf_ref, eq_ref, ek_ref, oq_ref, ok_ref, qx_ref, kx_ref, carry_ref, *, tm, inv_scale):
    @pl.when(pl.program_id(1) == 0)
    def _():
        carry_ref[...] = jnp.zeros_like(carry_ref)

    z = jnp.dot(x_ref[...], w_ref[...], preferred_element_type=F32) + bf_ref[...]
    log_f = jnp.minimum(z, 0.0) - jnp.log(1.0 + jnp.exp(-jnp.abs(z)))
    r = lax.broadcasted_iota(jnp.int32, (tm, tm), 0)
    c = lax.broadcasted_iota(jnp.int32, (tm, tm), 1)
    tri = (c <= r).astype(F32)
    csum = jnp.dot(tri, log_f, preferred_element_type=F32, precision=lax.Precision.HIGHEST) + carry_ref[...]
    carry_ref[...] = csum[tm - 1:tm, :]
    parts = jnp.concatenate(_split3_bf16(csum * inv_scale), axis=-1)
    qx_ref[...] = (jnp.dot(parts, eq_ref[...], preferred_element_type=F32) + oq_ref[...]).astype(BF16)
    kx_ref[...] = (jnp.dot(parts, ek_ref[...], preferred_element_type=F32) + ok_ref[...]).astype(BF16)


def _fox_bias_layout():
    gw = LANES // FOX_HEADS
    eq = np.zeros((3 * LANES, LANES), np.float32)
    ek = np.zeros((3 * LANES, LANES), np.float32)
    oq = np.zeros((1, LANES), np.float32)
    ok = np.zeros((1, LANES), np.float32)
    for h in range(FOX_HEADS):
        for t in range(3):
            eq[t * LANES + h, gw * h + t] = 1.0
            ek[t * LANES + h, gw * h + 3 + t] = -1.0
            oq[0, gw * h + 3 + t] = 1.0
            ok[0, gw * h + t] = 1.0
    return jnp.asarray(eq, BF16), jnp.asarray(ek, BF16), jnp.asarray(oq), jnp.asarray(ok), gw


def _fox_gate(x3, w_f, b_f, *, inv_scale, tm=256):
    bsz, seq, k = x3.shape
    tm = min(tm, seq)
    eq, ek, oq, ok, gw = _fox_bias_layout()
    const = lambda b, i: (0, 0)
    out_spec = pl.BlockSpec((None, tm, LANES), lambda b, i: (b, i, 0))
    qx, kx = pl.pallas_call(
        functools.partial(_fox_gate_kernel, tm=tm, inv_scale=inv_scale),
        grid=(bsz, seq // tm),
        in_specs=[pl.BlockSpec((None, tm, k), lambda b, i: (b, i, 0)),
                  pl.BlockSpec((k, LANES), const), pl.BlockSpec((1, LANES), const),
                  pl.BlockSpec((3 * LANES, LANES), const), pl.BlockSpec((3 * LANES, LANES), const),
                  pl.BlockSpec((1, LANES), const), pl.BlockSpec((1, LANES), const)],
        out_specs=[out_spec, out_spec],
        out_shape=[jax.ShapeDtypeStruct((bsz, seq, LANES), BF16)] * 2,
        scratch_shapes=[pltpu.VMEM((1, LANES), F32)],
        compiler_params=_params("parallel", "arbitrary"),
        name="fox_gate",
    )(x3, w_f, b_f, eq, ek, oq, ok)
    return qx, kx, gw


def _rms_rows(v, g):
    return v * lax.rsqrt(jnp.mean(v * v, axis=-1, keepdims=True) + RMS_EPS) * g


def _mla_down_kernel(x_ref, w_ref, gq_ref, gkv_ref, cos_ref, sa_ref, sb_ref, cq_ref, ckv_ref, kr_ref, *, half):
    acc = jnp.dot(x_ref[...], w_ref[...], preferred_element_type=F32)
    q_end, kv_end = MLA_Q_RANK, MLA_Q_RANK + MLA_KV_RANK
    cq_ref[...] = _rms_rows(acc[:, :q_end], gq_ref[...]).astype(BF16)
    ckv_ref[...] = _rms_rows(acc[:, q_end:kv_end], gkv_ref[...]).astype(BF16)
    kr_ref[...] = _apply_rot(acc[:, kv_end:], cos_ref[...], sa_ref[...], sb_ref[...], half).astype(BF16)


def _mla_down(x, w, gq, gkv, rot, *, tm=1024):
    m, k = x.shape
    n = w.shape[1]
    tm = min(tm, m)
    tabs, half = rot
    row = lambda i: (i, 0)
    const = lambda i: (0, 0)
    return pl.pallas_call(
        functools.partial(_mla_down_kernel, half=half),
        grid=(m // tm,),
        in_specs=[pl.BlockSpec((tm, k), row), pl.BlockSpec((k, n), const),
                  pl.BlockSpec((1, MLA_Q_RANK), const), pl.BlockSpec((1, MLA_KV_RANK), const)]
                 + [pl.BlockSpec((tm, LANES), row)] * 3,
        out_specs=[pl.BlockSpec((tm, MLA_Q_RANK), row), pl.BlockSpec((tm, MLA_KV_RANK), row),
                   pl.BlockSpec((tm, LANES), row)],
        out_shape=[jax.ShapeDtypeStruct((m, MLA_Q_RANK), BF16), jax.ShapeDtypeStruct((m, MLA_KV_RANK), BF16),
                   jax.ShapeDtypeStruct((m, LANES), BF16)],
        compiler_params=_params("parallel"),
        name="mla_down",
    )(x, w, gq.reshape(1, -1), gkv.reshape(1, -1), *tabs)


def _flash_kernel(*refs, tq, tk, c_exp, mode, extra_gw, heads_per_block, moba, lam_init):
    it = iter(refs)
    q_ref, k_ref, v_ref = next(it), next(it), next(it)
    qx_ref = kx_ref = km_ref = lamp_ref = subg_ref = None
    if extra_gw is not None:
        qx_ref, kx_ref = next(it), next(it)
    if moba:
        km_ref = next(it)
    if mode == "diff":
        lamp_ref, subg_ref = next(it), next(it)
    o_ref = next(it)
    m_scr, l_scr, acc_scr = next(it), next(it), next(it)

    hb = pl.program_id(1)
    qi = pl.program_id(2)
    lane = lax.broadcasted_iota(jnp.int32, (1, LANES), 1)
    q = q_ref[...]
    zero = jnp.zeros_like(q)
    q_half = [jnp.where(lane // HEAD_DIM == e, q, zero) for e in range(2)]

    if extra_gw is not None:
        groups = LANES // extra_gw
        qx = qx_ref[...]
        q_aug = []
        for e in range(2):
            grp = (hb * heads_per_block + e) % groups
            q_aug.append(jnp.concatenate([q_half[e], jnp.where(lane // extra_gw == grp, qx, jnp.zeros_like(qx))], -1))
    elif moba:
        km = km_ref[...]
        q_aug = []
        for e in range(2):
            gate = lax.dot_general(q_half[e], km, (((1,), (1,)), ((), ())), preferred_element_type=F32)
            cur = (qi * tq) // MOBA_BLOCK
            rank = jnp.zeros((tq, LANES), F32)
            for n in range(moba):
                gn = gate[:, n:n + 1]
                beats = (gn > gate) | ((gn == gate) & (n < lane))
                rank = rank + jnp.where(beats & (n < cur), 1.0, 0.0)
            keep = ((lane < cur) & (rank < MOBA_TOPK)) | (lane == cur)
            bias = jnp.where(keep, 0.0, NEG).astype(BF16)
            q_aug.append(jnp.concatenate([q_half[e], bias], -1))
    else:
        q_aug = q_half

    m_scr[...] = jnp.full_like(m_scr, NEG)
    l_scr[...] = jnp.zeros_like(l_scr)
    acc_scr[...] = jnp.zeros_like(acc_scr)

    def step(j, masked):
        start = pl.multiple_of(j * tk, tk)
        k = k_ref[pl.ds(start, tk), :]
        v = v_ref[pl.ds(start, tk), :]
        if extra_gw is not None:
            k = jnp.concatenate([k, kx_ref[pl.ds(start, tk), :]], -1)
        elif moba:
            blk = (start + lax.broadcasted_iota(jnp.int32, (tk, 1), 0)) // MOBA_BLOCK
            k = jnp.concatenate([k, jnp.where(blk == lane, 1.0, 0.0).astype(BF16)], -1)
        for e in range(2):
            s = lax.dot_general(q_aug[e], k, (((1,), (1,)), ((), ())), preferred_element_type=F32)
            if masked:
                r = qi * tq + lax.broadcasted_iota(jnp.int32, (tq, tk), 0)
                c = start + lax.broadcasted_iota(jnp.int32, (tq, tk), 1)
                s = jnp.where(c <= r, s, NEG)
            m_old = m_scr[e]
            m_new = jnp.maximum(m_old, jnp.max(s, axis=-1, keepdims=True))
            p = jnp.exp2((s - m_new) * c_exp)
            alpha = jnp.exp2((m_old - m_new) * c_exp)
            l_scr[e] = alpha * l_scr[e] + jnp.sum(p, axis=-1, keepdims=True)
            acc_scr[e] = alpha * acc_scr[e] + jnp.dot(p.astype(BF16), v, preferred_element_type=F32)
            m_scr[e] = m_new

    n_full = (qi * tq) // tk

    def body(j, carry):
        step(j, False)
        return carry

    lax.fori_loop(0, n_full, body, 0)
    step(n_full, True)

    o0 = acc_scr[0] / l_scr[0]
    o1 = acc_scr[1] / l_scr[1]
    if mode == "pair":
        o_ref[...] = jnp.where(lane < HEAD_DIM, o0, o1).astype(o_ref.dtype)
    else:
        lp = lamp_ref[...]
        lam = (jnp.exp(jnp.sum(lp[0:1] * lp[1:2], axis=-1, keepdims=True))
               - jnp.exp(jnp.sum(lp[2:3] * lp[3:4], axis=-1, keepdims=True)) + lam_init)
        o = o0 - lam * o1
        o = o * lax.rsqrt(jnp.mean(o * o, axis=-1, keepdims=True) + RMS_EPS) * subg_ref[...] * (1.0 - lam_init)
        o_ref[...] = o.astype(o_ref.dtype)


def _flash(q, k, v, *, qcol, kcol, vcol, n_blocks, c_exp, mode="pair", qx=None, qx_col=None, kx=None, extra_gw=None,
           km=None, lam_params=None, subg=None, lam_init=None, tq=256, tk=512):
    bsz, seq, _ = q.shape
    tq, tk = min(tq, seq), min(tk, seq)
    assert seq % tq == 0 and seq % tk == 0 and tk % tq == 0
    heads_per_block = 2 if mode == "pair" else 1
    in_specs = [pl.BlockSpec((None, tq, LANES), lambda b, h, i: (b, i, qcol + h)),
                pl.BlockSpec((None, seq, LANES), lambda b, h, i: (b, 0, kcol + h)),
                pl.BlockSpec((None, seq, LANES), lambda b, h, i: (b, 0, vcol + h))]
    args = [q, k, v]
    if extra_gw is not None:
        in_specs += [pl.BlockSpec((None, tq, LANES), lambda b, h, i: (b, i, qx_col(h))),
                     pl.BlockSpec((None, seq, LANES), lambda b, h, i: (b, 0, 0))]
        args += [qx, kx]
    moba = 0
    if km is not None:
        moba = seq // MOBA_BLOCK
        assert moba <= LANES and MOBA_BLOCK % tq == 0 and min(MOBA_TOPK, max(moba - 1, 1)) == MOBA_TOPK
        in_specs.append(pl.BlockSpec((None, LANES, LANES), lambda b, h, i: (b, 0, h)))
        args.append(km)
    if mode == "diff":
        in_specs += [pl.BlockSpec((4, HEAD_DIM), lambda b, h, i: (0, 0)), pl.BlockSpec((1, LANES), lambda b, h, i: (0, 0))]
        args += [lam_params, subg]
    return pl.pallas_call(
        functools.partial(_flash_kernel, tq=tq, tk=tk, c_exp=c_exp, mode=mode, extra_gw=extra_gw,
                          heads_per_block=heads_per_block, moba=moba, lam_init=lam_init),
        grid=(bsz, n_blocks, seq // tq),
        in_specs=in_specs,
        out_specs=pl.BlockSpec((None, tq, LANES), lambda b, h, i: (b, i, h)),
        out_shape=jax.ShapeDtypeStruct((bsz, seq, n_blocks * LANES), BF16),
        scratch_shapes=[pltpu.VMEM((2, tq, 1), F32), pltpu.VMEM((2, tq, 1), F32), pltpu.VMEM((2, tq, LANES), F32)],
        compiler_params=_params("parallel", "parallel", "arbitrary"),
        name="flash_" + mode + ("_x" if extra_gw else "") + ("_moba" if moba else ""),
    )(*args)


def _diff_lambda_init(layer):
    return 0.8 - 0.6 * math.exp(-0.3 * layer)


def _diff_mixer(h16, bsz, seq, rot_p, w_qkv, lq1, lk1, lq2, lk2, subln_g, layer):
    width = DIFF_HEADS * 2 * HEAD_DIM
    wb = w_qkv.astype(BF16)
    qk = _proj(h16, wb[:, :2 * width], rot=rot_p)
    v = _proj(h16, wb[:, 2 * width:])
    nb = width // LANES
    lam_params = jnp.stack([lq1, lk1, lq2, lk2]).astype(F32)
    o = _flash(qk.reshape(bsz, seq, -1), qk.reshape(bsz, seq, -1), v.reshape(bsz, seq, -1),
               qcol=0, kcol=nb, vcol=0, n_blocks=nb, c_exp=HEAD_DIM ** -0.5 * LOG2E, mode="diff",
               lam_params=lam_params, subg=subln_g.reshape(1, LANES).astype(F32), lam_init=_diff_lambda_init(layer))
    return o.reshape(bsz * seq, width)


def _fox_mixer(h16, bsz, seq, w_in, b_f):
    width = FOX_HEADS * HEAD_DIM
    scale = HEAD_DIM ** -0.5
    wb = w_in.astype(BF16)
    qkv = _proj(h16, wb[:, :3 * width])
    w_f = jnp.pad(wb[:, 3 * width:], ((0, 0), (0, LANES - FOX_HEADS)))
    bf = jnp.pad(b_f.astype(F32), (0, LANES - FOX_HEADS)).reshape(1, LANES)
    qx, kx, gw = _fox_gate(h16.reshape(bsz, seq, -1), w_f, bf, inv_scale=1.0 / scale)
    nb = width // LANES
    qkv3 = qkv.reshape(bsz, seq, -1)
    o = _flash(qkv3, qkv3, qkv3, qcol=0, kcol=nb, vcol=2 * nb, n_blocks=nb, c_exp=scale * LOG2E,
               qx=qx, qx_col=lambda h: 0, kx=kx, extra_gw=gw)
    return o.reshape(bsz * seq, width)


def _mla_mixer(h16, bsz, seq, rot_m, w_down, gq, gkv, w_uq, w_ukv):
    nope, rope, vd, heads = HEAD_DIM, MLA_ROPE, HEAD_DIM, MLA_HEADS
    wd = w_down.astype(BF16)
    kv_end = MLA_Q_RANK + MLA_KV_RANK
    wd = jnp.concatenate([wd[:, :kv_end]] + [wd[:, kv_end:]] * (LANES // rope), axis=1)
    cq, ckv, kr = _mla_down(h16, wd, gq.astype(F32), gkv.astype(F32), rot_m)
    wq = w_uq.astype(BF16).reshape(MLA_Q_RANK, heads, nope + rope)
    wq = jnp.concatenate([wq[:, :, :nope].reshape(MLA_Q_RANK, -1), wq[:, :, nope:].reshape(MLA_Q_RANK, -1)], axis=1)
    wkv = w_ukv.astype(BF16).reshape(MLA_KV_RANK, heads, nope + vd)
    wkv = jnp.concatenate([wkv[:, :, :nope].reshape(MLA_KV_RANK, -1), wkv[:, :, nope:].reshape(MLA_KV_RANK, -1)], axis=1)
    q = _proj(cq, wq, rot=rot_m, rot_from=heads * nope)
    kv = _proj(ckv, wkv)
    nb = heads * nope // LANES
    groups = LANES // rope
    o = _flash(q.reshape(bsz, seq, -1), kv.reshape(bsz, seq, -1), kv.reshape(bsz, seq, -1),
               qcol=0, kcol=0, vcol=nb, n_blocks=nb, c_exp=(nope + rope) ** -0.5 * LOG2E,
               qx=q.reshape(bsz, seq, -1), qx_col=lambda h: nb + (2 * h) // groups, kx=kr.reshape(bsz, seq, LANES),
               extra_gw=rope)
    return o.reshape(bsz * seq, heads * vd)


def _moba_mixer(h16, bsz, seq, rot_p, w_qkv):
    width = MOBA_HEADS * HEAD_DIM
    wb = w_qkv.astype(BF16)
    qk, kmean = _proj(h16, wb[:, :2 * width], rot=rot_p, want_kmean=True)
    v = _proj(h16, wb[:, 2 * width:])
    n_kb = seq // MOBA_BLOCK
    km = kmean.reshape(bsz, n_kb, 2 * width)[:, :, width:].astype(BF16)
    km = jnp.pad(km, ((0, 0), (0, LANES - n_kb), (0, 0)))
    nb = width // LANES
    qk3 = qk.reshape(bsz, seq, -1)
    o = _flash(qk3, qk3, v.reshape(bsz, seq, -1), qcol=0, kcol=nb, vcol=0, n_blocks=nb,
               c_exp=HEAD_DIM ** -0.5 * LOG2E, km=km)
    return o.reshape(bsz * seq, width)


def kernel(x, positions, diff_w_qkv, diff_lambda_q1, diff_lambda_k1, diff_lambda_q2, diff_lambda_k2, diff_subln_g,
           diff_w_o, fox_w_in, fox_b_f, fox_w_o, mla_w_down, mla_q_norm_g, mla_kv_norm_g, mla_w_uq, mla_w_ukv,
           mla_w_o, moba_w_qkv, moba_w_o, ffn_w_in, ffn_conv_w, ffn_conv_b, ffn_w_out, ln1_g, ln1_b, ln2_g, ln2_b):
    bsz, seq, d_model = x.shape
    depth = ffn_w_in.shape[0]
    alpha = (2 * depth) ** 0.25
    rot_p = _rot_tables(positions, PARTIAL_ROT, HEAD_DIM)
    rot_m = _rot_tables(positions, MLA_ROPE, MLA_ROPE)
    h32 = x.reshape(bsz * seq, d_model).astype(F32)
    h16 = h32.astype(BF16)
    for i in range(depth):
        m, u = i % N_MIXERS, i // N_MIXERS
        if m == 0:
            y = _diff_mixer(h16, bsz, seq, rot_p, diff_w_qkv[u], diff_lambda_q1[u], diff_lambda_k1[u],
                            diff_lambda_q2[u], diff_lambda_k2[u], diff_subln_g[u], i)
            w_o = diff_w_o[u]
        elif m == 1:
            y = _fox_mixer(h16, bsz, seq, fox_w_in[u], fox_b_f[u])
            w_o = fox_w_o[u]
        elif m == 2:
            y = _mla_mixer(h16, bsz, seq, rot_m, mla_w_down[u], mla_q_norm_g[u], mla_kv_norm_g[u], mla_w_uq[u],
                           mla_w_ukv[u])
            w_o = mla_w_o[u]
        else:
            y = _moba_mixer(h16, bsz, seq, rot_p, moba_w_qkv[u])
            w_o = moba_w_o[u]
        h32, h16 = _proj_ln(y, w_o.astype(BF16), h32, ln1_g[i].astype(F32), ln1_b[i].astype(F32), alpha=alpha)
        gu = _proj(h16, ffn_w_in[i].astype(BF16), out_dtype=F32)
        h32, h16 = _ffn_out(gu, ffn_conv_w[i].astype(F32), ffn_conv_b[i].astype(F32), ffn_w_out[i].astype(BF16),
                            h32, ln2_g[i].astype(F32), ln2_b[i].astype(F32), alpha=alpha, seq=seq)
    return h32.reshape(bsz, seq, d_model).astype(x.dtype)
```

```python
import functools
import math

import jax
import jax.numpy as jnp
import numpy as np
from jax import lax
from jax.experimental import pallas as pl
from jax.experimental.pallas import tpu as pltpu

F32 = jnp.float32
BF16 = jnp.bfloat16

LANES = 128
VMEM_LIMIT = 56 * 1024 * 1024

ROPE_THETA = 500000.0
LN_EPS = 1e-5
RMS_EPS = 1e-6
HEAD_DIM = 64
PARTIAL_ROT = HEAD_DIM // 4
DIFF_HEADS = 8
FOX_HEADS = 16
MLA_HEADS = 16
MLA_ROPE = 32
MLA_Q_RANK = 384
MLA_KV_RANK = 256
MOBA_HEADS = 16
MOBA_BLOCK = 256
MOBA_TOPK = 3
CONV_WIDTH = 3
N_MIXERS = 4

ONES_ROWS = 8
NEG = -1e30
LOG2E = math.log2(math.e)


def _params(*sem):
    return pltpu.CompilerParams(dimension_semantics=sem, vmem_limit_bytes=VMEM_LIMIT)


def _rot_tables(positions, rot_dim, period):
    half = rot_dim // 2
    inv_freq = ROPE_THETA ** (-jnp.arange(0, rot_dim, 2, dtype=F32) / rot_dim)
    ang = positions.astype(F32)[..., None] * inv_freq
    cos, sin = jnp.cos(ang), jnp.sin(ang)
    lane = np.arange(LANES)
    j = lane % period
    idx = j % half
    lo = j < half
    hi = (j >= half) & (j < rot_dim)
    cos_t = jnp.where(lo | hi, cos[..., idx], 1.0)
    sin_a = jnp.where(lo, -sin[..., idx], 0.0)
    sin_b = jnp.where(hi, sin[..., idx], 0.0)
    rows = positions.shape[0] * positions.shape[1]
    return tuple(t.reshape(rows, LANES) for t in (cos_t, sin_a, sin_b)), half


def _apply_rot(val, cos, sin_a, sin_b, half):
    pieces = []
    for c in range(val.shape[1] // LANES):
        blk = val[:, c * LANES:(c + 1) * LANES]
        pieces.append(blk * cos + pltpu.roll(blk, LANES - half, 1) * sin_a + pltpu.roll(blk, half, 1) * sin_b)
    return pieces[0] if len(pieces) == 1 else jnp.concatenate(pieces, axis=-1)


def _proj_kernel(*refs, rot_half, rot_tile0, want_kmean, tm, scale, scale_tiles):
    x_ref, w_ref = refs[:2]
    rest = refs[2:]
    if rot_half is not None:
        cos_ref, sa_ref, sb_ref = rest[:3]
        rest = rest[3:]
    o_ref = rest[0]
    km_ref = rest[1] if want_kmean else None
    acc = jnp.dot(x_ref[...], w_ref[...], preferred_element_type=F32)

    def emit(val):
        if want_kmean:
            tn = val.shape[1]
            km_ref[...] = jnp.mean(val.reshape(tm // MOBA_BLOCK, MOBA_BLOCK, tn), axis=1)
        if scale is not None:
            val = val * jnp.where(pl.program_id(1) < scale_tiles, scale, 1.0)
        o_ref[...] = val.astype(o_ref.dtype)

    if rot_half is None:
        emit(acc)
        return

    def rotated():
        return _apply_rot(acc, cos_ref[...], sa_ref[...], sb_ref[...], rot_half)

    if rot_tile0 == 0:
        emit(rotated())
    else:
        j = pl.program_id(1)

        @pl.when(j >= rot_tile0)
        def _():
            emit(rotated())

        @pl.when(j < rot_tile0)
        def _():
            emit(acc)


def _proj(x, w, *, out_dtype=BF16, rot=None, rot_from=0, want_kmean=False, scale=None, scale_to=0, tm=1024, tn=512):
    m, k = x.shape
    n = w.shape[1]
    tm, tn = min(tm, m), min(tn, n)
    assert m % tm == 0 and n % tn == 0 and rot_from % tn == 0 and scale_to % tn == 0
    in_specs = [pl.BlockSpec((tm, k), lambda i, j: (i, 0)), pl.BlockSpec((k, tn), lambda i, j: (0, j))]
    args = [x, w]
    rot_half = None
    if rot is not None:
        tabs, rot_half = rot
        in_specs += [pl.BlockSpec((tm, LANES), lambda i, j: (i, 0))] * 3
        args += list(tabs)
    out_shape = [jax.ShapeDtypeStruct((m, n), out_dtype)]
    out_specs = [pl.BlockSpec((tm, tn), lambda i, j: (i, j))]
    if want_kmean:
        assert tm % MOBA_BLOCK == 0
        out_shape.append(jax.ShapeDtypeStruct((m // tm, tm // MOBA_BLOCK, n), F32))
        out_specs.append(pl.BlockSpec((None, tm // MOBA_BLOCK, tn), lambda i, j: (i, 0, j)))
    res = pl.pallas_call(
        functools.partial(_proj_kernel, rot_half=rot_half, rot_tile0=rot_from // tn, want_kmean=want_kmean, tm=tm,
                          scale=scale, scale_tiles=scale_to // tn),
        grid=(m // tm, n // tn),
        in_specs=in_specs,
        out_specs=out_specs,
        out_shape=out_shape,
        compiler_params=_params("parallel", "arbitrary"),
        name="proj",
    )(*args)
    return res if want_kmean else res[0]


def _layer_norm_rows(z, g, b):
    mu = jnp.mean(z, axis=-1, keepdims=True)
    zc = z - mu
    var = jnp.mean(zc * zc, axis=-1, keepdims=True)
    return zc * lax.rsqrt(var + LN_EPS) * g + b


def _proj_ln_kernel(x_ref, w_ref, res_ref, g_ref, b_ref, h32_ref, h16_ref, *, alpha):
    y = jnp.dot(x_ref[...], w_ref[...], preferred_element_type=F32)
    h = _layer_norm_rows(alpha * res_ref[...] + y, g_ref[...], b_ref[...])
    h32_ref[...] = h
    h16_ref[...] = h.astype(BF16)


def _proj_ln(x, w, res, g, b, *, alpha, tm=512):
    m, k = x.shape
    n = w.shape[1]
    tm = min(tm, m)
    row = lambda i: (i, 0)
    const = lambda i: (0, 0)
    return pl.pallas_call(
        functools.partial(_proj_ln_kernel, alpha=alpha),
        grid=(m // tm,),
        in_specs=[pl.BlockSpec((tm, k), row), pl.BlockSpec((k, n), const), pl.BlockSpec((tm, n), row),
                  pl.BlockSpec((1, n), const), pl.BlockSpec((1, n), const)],
        out_specs=[pl.BlockSpec((tm, n), row), pl.BlockSpec((tm, n), row)],
        out_shape=[jax.ShapeDtypeStruct((m, n), F32), jax.ShapeDtypeStruct((m, n), BF16)],
        compiler_params=_params("parallel"),
        name="proj_ln",
    )(x, w, res, g.reshape(1, n), b.reshape(1, n))


def _ffn_out_kernel(gate_ref, up_ref, prev_ref, cw_ref, cb_ref, w_ref, res_ref, g_ref, b_ref, h32_ref, h16_ref,
                    *, alpha, tm, seq, chunk):
    i = pl.program_id(0)
    at_seq_start = (i * tm) % seq == 0
    d_ff = gate_ref.shape[1]
    row = lax.broadcasted_iota(jnp.int32, (tm, 1), 0)
    acc = jnp.zeros((tm, w_ref.shape[1]), F32)
    for c in range(d_ff // chunk):
        cs = slice(c * chunk, (c + 1) * chunk)
        gt = gate_ref[:, cs]
        prev = jnp.where(at_seq_start, 0.0, prev_ref[:, cs])
        g1 = jnp.where(row == 0, prev[7:8, :], pltpu.roll(gt, 1, 0))
        g2 = jnp.where(row == 0, prev[6:7, :], jnp.where(row == 1, prev[7:8, :], pltpu.roll(gt, 2, 0)))
        cw = cw_ref[:, cs]
        conv = cw[0:1, :] * g2 + cw[1:2, :] * g1 + cw[2:3, :] * gt + cb_ref[:, cs]
        act = conv * (1.0 / (1.0 + jnp.exp(-conv))) * up_ref[:, cs]
        acc = acc + jnp.dot(act.astype(BF16), w_ref[cs, :], preferred_element_type=F32)
    h = _layer_norm_rows(alpha * res_ref[...] + acc, g_ref[...], b_ref[...])
    h32_ref[...] = h
    h16_ref[...] = h.astype(BF16)


def _ffn_out(gu, conv_w, conv_b, w_out, res, g, b, *, alpha, seq, tm=256, chunk=256):
    m = gu.shape[0]
    d_ff, n = w_out.shape
    tm = min(tm, m)
    assert seq % tm == 0 and tm % 8 == 0 and d_ff % chunk == 0
    row = lambda i: (i, 0)
    const = lambda i: (0, 0)
    sub = tm // 8
    return pl.pallas_call(
        functools.partial(_ffn_out_kernel, alpha=alpha, tm=tm, seq=seq, chunk=chunk),
        grid=(m // tm,),
        in_specs=[pl.BlockSpec((tm, d_ff), lambda i: (i, 0)),
                  pl.BlockSpec((tm, d_ff), lambda i: (i, 1)),
                  pl.BlockSpec((8, d_ff), lambda i: (jnp.maximum(i * sub - 1, 0), 0)),
                  pl.BlockSpec((CONV_WIDTH, d_ff), const), pl.BlockSpec((1, d_ff), const),
                  pl.BlockSpec((d_ff, n), const), pl.BlockSpec((tm, n), row),
                  pl.BlockSpec((1, n), const), pl.BlockSpec((1, n), const)],
        out_specs=[pl.BlockSpec((tm, n), row), pl.BlockSpec((tm, n), row)],
        out_shape=[jax.ShapeDtypeStruct((m, n), F32), jax.ShapeDtypeStruct((m, n), BF16)],
        compiler_params=_params("parallel"),
        name="ffn_out",
    )(gu, gu, gu, conv_w, conv_b.reshape(1, d_ff), w_out, res, g.reshape(1, n), b.reshape(1, n))


def _split3_bf16(v):
    hi = v.astype(BF16)
    r1 = v - hi.astype(F32)
    mid = r1.astype(BF16)
    lo = (r1 - mid.astype(F32)).astype(BF16)
    return hi, mid, lo


def _fox_gate_kernel(x_ref, w_ref, bf_ref, eq_ref, ek_ref, oq_ref, ok_ref, qx_ref, kx_ref, carry_ref, *, tm, inv_scale):
    @pl.when(pl.program_id(1) == 0)
    def _():
        carry_ref[...] = jnp.zeros_like(carry_ref)

    z = jnp.dot(x_ref[...], w_ref[...], preferred_element_type=F32) + bf_ref[...]
    log_f = jnp.minimum(z, 0.0) - jnp.log(1.0 + jnp.exp(-jnp.abs(z)))
    r = lax.broadcasted_iota(jnp.int32, (tm, tm), 0)
    c = lax.broadcasted_iota(jnp.int32, (tm, tm), 1)
    tri = (c <= r).astype(F32)
    csum = jnp.dot(tri, log_f, preferred_element_type=F32, precision=lax.Precision.HIGHEST) + carry_ref[...]
    carry_ref[...] = csum[tm - 1:tm, :]
    parts = jnp.concatenate(_split3_bf16(csum * inv_scale), axis=-1)
    qx_ref[...] = (jnp.dot(parts, eq_ref[...], preferred_element_type=F32) + oq_ref[...]).astype(BF16)
    kx_ref[...] = (jnp.dot(parts, ek_ref[...], preferred_element_type=F32) + ok_ref[...]).astype(BF16)


def _fox_bias_layout():
    gw = LANES // FOX_HEADS
    eq = np.zeros((3 * LANES, LANES), np.float32)
    ek = np.zeros((3 * LANES, LANES), np.float32)
    oq = np.zeros((1, LANES), np.float32)
    ok = np.zeros((1, LANES), np.float32)
    for h in range(FOX_HEADS):
        for t in range(3):
            eq[t * LANES + h, gw * h + t] = 1.0
            ek[t * LANES + h, gw * h + 3 + t] = -1.0
            oq[0, gw * h + 3 + t] = 1.0
            ok[0, gw * h + t] = 1.0
    return jnp.asarray(eq, BF16), jnp.asarray(ek, BF16), jnp.asarray(oq), jnp.asarray(ok), gw


def _fox_gate(x3, w_f, b_f, *, inv_scale, tm=256):
    bsz, seq, k = x3.shape
    tm = min(tm, seq)
    eq, ek, oq, ok, gw = _fox_bias_layout()
    const = lambda b, i: (0, 0)
    out_spec = pl.BlockSpec((None, tm, LANES), lambda b, i: (b, i, 0))
    qx, kx = pl.pallas_call(
        functools.partial(_fox_gate_kernel, tm=tm, inv_scale=inv_scale),
        grid=(bsz, seq // tm),
        in_specs=[pl.BlockSpec((None, tm, k), lambda b, i: (b, i, 0)),
                  pl.BlockSpec((k, LANES), const), pl.BlockSpec((1, LANES), const),
                  pl.BlockSpec((3 * LANES, LANES), const), pl.BlockSpec((3 * LANES, LANES), const),
                  pl.BlockSpec((1, LANES), const), pl.BlockSpec((1, LANES), const)],
        out_specs=[out_spec, out_spec],
        out_shape=[jax.ShapeDtypeStruct((bsz, seq, LANES), BF16)] * 2,
        scratch_shapes=[pltpu.VMEM((1, LANES), F32)],
        compiler_params=_params("parallel", "arbitrary"),
        name="fox_gate",
    )(x3, w_f, b_f, eq, ek, oq, ok)
    return qx, kx, gw


def _rms_rows(v, g):
    return v * lax.rsqrt(jnp.mean(v * v, axis=-1, keepdims=True) + RMS_EPS) * g


def _mla_down_kernel(x_ref, w_ref, gq_ref, gkv_ref, cos_ref, sa_ref, sb_ref, cq_ref, ckv_ref, kr_ref, *, half):
    acc = jnp.dot(x_ref[...], w_ref[...], preferred_element_type=F32)
    q_end, kv_end = MLA_Q_RANK, MLA_Q_RANK + MLA_KV_RANK
    cq_ref[...] = _rms_rows(acc[:, :q_end], gq_ref[...]).astype(BF16)
    ckv_ref[...] = _rms_rows(acc[:, q_end:kv_end], gkv_ref[...]).astype(BF16)
    kr_ref[...] = _apply_rot(acc[:, kv_end:], cos_ref[...], sa_ref[...], sb_ref[...], half).astype(BF16)


def _mla_down(x, w, gq, gkv, rot, *, tm=1024):
    m, k = x.shape
    n = w.shape[1]
    tm = min(tm, m)
    tabs, half = rot
    row = lambda i: (i, 0)
    const = lambda i: (0, 0)
    return pl.pallas_call(
        functools.partial(_mla_down_kernel, half=half),
        grid=(m // tm,),
        in_specs=[pl.BlockSpec((tm, k), row), pl.BlockSpec((k, n), const),
                  pl.BlockSpec((1, MLA_Q_RANK), const), pl.BlockSpec((1, MLA_KV_RANK), const)]
                 + [pl.BlockSpec((tm, LANES), row)] * 3,
        out_specs=[pl.BlockSpec((tm, MLA_Q_RANK), row), pl.BlockSpec((tm, MLA_KV_RANK), row),
                   pl.BlockSpec((tm, LANES), row)],
        out_shape=[jax.ShapeDtypeStruct((m, MLA_Q_RANK), BF16), jax.ShapeDtypeStruct((m, MLA_KV_RANK), BF16),
                   jax.ShapeDtypeStruct((m, LANES), BF16)],
        compiler_params=_params("parallel"),
        name="mla_down",
    )(x, w, gq.reshape(1, -1), gkv.reshape(1, -1), *tabs)


def _flash_kernel(*refs, tq, tk, mode, extra_gw, heads_per_block, moba, lam_init):
    it = iter(refs)
    q_ref, k_ref, vt_ref = next(it), next(it), next(it)
    qx_ref = kx_ref = km_ref = lamp_ref = subg_ref = g_scr = None
    if extra_gw is not None:
        qx_ref, kx_ref = next(it), next(it)
    if moba:
        km_ref = next(it)
    if mode == "diff":
        lamp_ref, subg_ref = next(it), next(it)
    o_ref = next(it)
    m_scr, acc_scr, s_scr = next(it), next(it), next(it)
    if moba:
        g_scr = next(it)

    hb = pl.program_id(1)
    qi = pl.program_id(2)
    lane = lax.broadcasted_iota(jnp.int32, (1, LANES), 1)
    q = q_ref[...]
    zero = jnp.zeros_like(q)
    q_half = [jnp.where(lane // HEAD_DIM == e, q, zero) for e in range(2)]

    if extra_gw is not None:
        groups = LANES // extra_gw
        qx = qx_ref[...]
        q_aug = []
        for e in range(2):
            grp = (hb * heads_per_block + e) % groups
            q_aug.append(jnp.concatenate([q_half[e], jnp.where(lane // extra_gw == grp, qx, jnp.zeros_like(qx))], -1))
    elif moba:
        q_aug = _moba_bias(q_half, km_ref, g_scr, qi, tq, moba)
    else:
        q_aug = q_half

    m_scr[...] = jnp.full_like(m_scr, NEG)
    acc_scr[...] = jnp.zeros_like(acc_scr)
    row = lax.broadcasted_iota(jnp.int32, (LANES, 1), 0)

    def scores(slot, j):
        start = pl.multiple_of(j * tk, tk)
        k = k_ref[pl.ds(start, tk), :]
        if extra_gw is not None:
            k = jnp.concatenate([k, kx_ref[pl.ds(start, tk), :]], -1)
        elif moba:
            blk = (start + lax.broadcasted_iota(jnp.int32, (tk, 1), 0)) // MOBA_BLOCK
            k = jnp.concatenate([k, jnp.where(blk == lane, 1.0, 0.0).astype(BF16)], -1)
        for e in range(2):
            s_scr[slot, e] = lax.dot_general(k, q_aug[e], (((1,), (1,)), ((), ())), preferred_element_type=F32)

    def consume(slot, j, masked):
        vt = vt_ref[j]
        one = jnp.ones_like(vt)
        if mode == "pair":
            vt_e = [jnp.where(row // HEAD_DIM == e, vt, one) for e in range(2)]
        else:
            vt_e = [jnp.concatenate([vt, one[:ONES_ROWS]], 0)] * 2
        for e in range(2):
            se = s_scr[slot, e]
            if masked:
                key = j * tk + lax.broadcasted_iota(jnp.int32, (tk, tq), 0)
                qry = qi * tq + lax.broadcasted_iota(jnp.int32, (tk, tq), 1)
                se = jnp.where(key <= qry, se, NEG)
            m_old = m_scr[e]
            m_new = jnp.maximum(m_old, jnp.max(se, axis=0, keepdims=True))
            p = jnp.exp2(se - m_new)
            alpha = jnp.exp2(m_old - m_new)
            acc_scr[e] = alpha * acc_scr[e] + jnp.dot(vt_e[e], p.astype(BF16), preferred_element_type=F32)
            m_scr[e] = m_new

    n_full = (qi * tq) // tk
    scores(0, 0)

    def body(t, carry):
        j = 2 * t
        scores(1, j + 1)
        consume(0, j, False)
        scores(0, j + 2)
        consume(1, j + 1, False)
        return carry

    lax.fori_loop(0, n_full // 2, body, 0)

    @pl.when(n_full % 2 == 1)
    def _():
        scores(1, n_full)
        consume(0, n_full - 1, False)
        consume(1, n_full, True)

    @pl.when(n_full % 2 == 0)
    def _():
        consume(0, n_full, True)

    if mode == "pair":
        a0, a1 = acc_scr[0], acc_scr[1]
        o_t = jnp.concatenate([a0[:HEAD_DIM] / a0[HEAD_DIM:HEAD_DIM + 1], a1[HEAD_DIM:] / a1[0:1]], axis=0)
        o_ref[...] = o_t.T.astype(o_ref.dtype)
    else:
        o0 = (acc_scr[0, :LANES, :] / acc_scr[0, LANES:LANES + 1, :]).T
        o1 = (acc_scr[1, :LANES, :] / acc_scr[1, LANES:LANES + 1, :]).T
        lp = lamp_ref[...]
        lam = (jnp.exp(jnp.sum(lp[0:1] * lp[1:2], axis=-1, keepdims=True))
               - jnp.exp(jnp.sum(lp[2:3] * lp[3:4], axis=-1, keepdims=True)) + lam_init)
        o = o0 - lam * o1
        o = o * lax.rsqrt(jnp.mean(o * o, axis=-1, keepdims=True) + RMS_EPS) * subg_ref[...] * (1.0 - lam_init)
        o_ref[...] = o.astype(o_ref.dtype)


def _moba_bias(q_half, km_ref, g_scr, qi, tq, n_blocks):
    rows = max(8, -(-n_blocks // 8) * 8)
    km = km_ref[...]
    blk = lax.broadcasted_iota(jnp.int32, (LANES, 1), 0)
    cur = (qi * tq + lax.broadcasted_iota(jnp.int32, (1, tq), 1)) // MOBA_BLOCK
    for e in range(2):
        gate = lax.dot_general(km, q_half[e], (((1,), (1,)), ((), ())), preferred_element_type=F32)
        g_scr[e] = jnp.where(blk < cur, gate, NEG)
    blk_r = blk[:rows]

    def count(n, ranks):
        out = []
        for e in range(2):
            gn = g_scr[e, pl.ds(n, 1), :]
            g = g_scr[e, :rows, :]
            beats = (gn > g) | ((gn == g) & (n < blk_r))
            out.append(ranks[e] + jnp.where(beats, 1.0, 0.0))
        return tuple(out)

    n_past = (qi * tq + tq - 1) // MOBA_BLOCK
    ranks = lax.fori_loop(0, n_past, count, (jnp.zeros((rows, tq), F32),) * 2)
    q_aug = []
    for e in range(2):
        keep = ((blk_r < cur) & (ranks[e] < MOBA_TOPK)) | (blk_r == cur)
        bias_t = jnp.where(keep, 0.0, NEG)
        if rows < LANES:
            bias_t = jnp.concatenate([bias_t, jnp.zeros((LANES - rows, tq), F32)], axis=0)
        q_aug.append(jnp.concatenate([q_half[e], bias_t.T.astype(BF16)], -1))
    return q_aug


def _key_major(v, tk):
    bsz, seq, c = v.shape
    return v.reshape(bsz, seq // tk, tk, c).transpose(0, 1, 3, 2)


def _flash(q, k, vt, *, qcol, kcol, vcol, n_blocks, mode="pair", qx=None, qx_col=None, kx=None, extra_gw=None,
           km=None, lam_params=None, subg=None, lam_init=None, tq=512, tk=512):
    bsz, seq, _ = q.shape
    tq, tk = min(tq, seq), min(tk, seq)
    assert seq % tq == 0 and seq % tk == 0 and tk % tq == 0
    heads_per_block = 2 if mode == "pair" else 1
    in_specs = [pl.BlockSpec((None, tq, LANES), lambda b, h, i: (b, i, qcol + h)),
                pl.BlockSpec((None, seq, LANES), lambda b, h, i: (b, 0, kcol + h)),
                pl.BlockSpec((None, seq // tk, LANES, tk), lambda b, h, i: (b, 0, vcol + h, 0))]
    args = [q, k, _key_major(vt, tk)]
    acc_rows = LANES if mode == "pair" else LANES + ONES_ROWS
    scratch = [pltpu.VMEM((2, 1, tq), F32), pltpu.VMEM((2, acc_rows, tq), F32), pltpu.VMEM((2, 2, tk, tq), F32)]
    if extra_gw is not None:
        in_specs += [pl.BlockSpec((None, tq, LANES), lambda b, h, i: (b, i, qx_col(h))),
                     pl.BlockSpec((None, seq, LANES), lambda b, h, i: (b, 0, 0))]
        args += [qx, kx]
    moba = 0
    if km is not None:
        moba = seq // MOBA_BLOCK
        assert moba <= LANES and seq % MOBA_BLOCK == 0 and min(MOBA_TOPK, max(moba - 1, 1)) == MOBA_TOPK
        in_specs.append(pl.BlockSpec((None, LANES, LANES), lambda b, h, i: (b, 0, h)))
        args.append(km)
        scratch.append(pltpu.VMEM((2, LANES, tq), F32))
    if mode == "diff":
        in_specs += [pl.BlockSpec((4, HEAD_DIM), lambda b, h, i: (0, 0)), pl.BlockSpec((1, LANES), lambda b, h, i: (0, 0))]
        args += [lam_params, subg]
    return pl.pallas_call(
        functools.partial(_flash_kernel, tq=tq, tk=tk, mode=mode, extra_gw=extra_gw,
                          heads_per_block=heads_per_block, moba=moba, lam_init=lam_init),
        grid=(bsz, n_blocks, seq // tq),
        in_specs=in_specs,
        out_specs=pl.BlockSpec((None, tq, LANES), lambda b, h, i: (b, i, h)),
        out_shape=jax.ShapeDtypeStruct((bsz, seq, n_blocks * LANES), BF16),
        scratch_shapes=scratch,
        compiler_params=_params("parallel", "parallel", "arbitrary"),
        name="flash_" + mode + ("_x" if extra_gw else "") + ("_moba" if moba else ""),
    )(*args)


def _diff_lambda_init(layer):
    return 0.8 - 0.6 * math.exp(-0.3 * layer)


def _diff_mixer(h16, bsz, seq, rot_p, w_qkv, lq1, lk1, lq2, lk2, subln_g, layer):
    width = DIFF_HEADS * 2 * HEAD_DIM
    wb = w_qkv.astype(BF16)
    qk = _proj(h16, wb[:, :2 * width], rot=rot_p, scale=HEAD_DIM ** -0.5 * LOG2E, scale_to=width)
    v = _proj(h16, wb[:, 2 * width:])
    nb = width // LANES
    lam_params = jnp.stack([lq1, lk1, lq2, lk2]).astype(F32)
    o = _flash(qk.reshape(bsz, seq, -1), qk.reshape(bsz, seq, -1), v.reshape(bsz, seq, -1),
               qcol=0, kcol=nb, vcol=0, n_blocks=nb, mode="diff",
               lam_params=lam_params, subg=subln_g.reshape(1, LANES).astype(F32), lam_init=_diff_lambda_init(layer))
    return o.reshape(bsz * seq, width)


def _fox_mixer(h16, bsz, seq, w_in, b_f):
    width = FOX_HEADS * HEAD_DIM
    scale = HEAD_DIM ** -0.5
    wb = w_in.astype(BF16)
    qkv = _proj(h16, wb[:, :3 * width], scale=scale * LOG2E, scale_to=width)
    w_f = jnp.pad(wb[:, 3 * width:], ((0, 0), (0, LANES - FOX_HEADS)))
    bf = jnp.pad(b_f.astype(F32), (0, LANES - FOX_HEADS)).reshape(1, LANES)
    qx, kx, gw = _fox_gate(h16.reshape(bsz, seq, -1), w_f, bf, inv_scale=LOG2E)
    nb = width // LANES
    qkv3 = qkv.reshape(bsz, seq, -1)
    o = _flash(qkv3, qkv3, qkv3[..., 2 * width:], qcol=0, kcol=nb, vcol=0, n_blocks=nb,
               qx=qx, qx_col=lambda h: 0, kx=kx, extra_gw=gw)
    return o.reshape(bsz * seq, width)


def _mla_mixer(h16, bsz, seq, rot_m, w_down, gq, gkv, w_uq, w_ukv):
    nope, rope, vd, heads = HEAD_DIM, MLA_ROPE, HEAD_DIM, MLA_HEADS
    wd = w_down.astype(BF16)
    kv_end = MLA_Q_RANK + MLA_KV_RANK
    wd = jnp.concatenate([wd[:, :kv_end]] + [wd[:, kv_end:]] * (LANES // rope), axis=1)
    cq, ckv, kr = _mla_down(h16, wd, gq.astype(F32), gkv.astype(F32), rot_m)
    wq = w_uq.astype(BF16).reshape(MLA_Q_RANK, heads, nope + rope)
    wq = jnp.concatenate([wq[:, :, :nope].reshape(MLA_Q_RANK, -1), wq[:, :, nope:].reshape(MLA_Q_RANK, -1)], axis=1)
    wkv = w_ukv.astype(BF16).reshape(MLA_KV_RANK, heads, nope + vd)
    wkv = jnp.concatenate([wkv[:, :, :nope].reshape(MLA_KV_RANK, -1), wkv[:, :, nope:].reshape(MLA_KV_RANK, -1)], axis=1)
    q = _proj(cq, wq, rot=rot_m, rot_from=heads * nope, scale=(nope + rope) ** -0.5 * LOG2E, scale_to=wq.shape[1])
    kv = _proj(ckv, wkv)
    nb = heads * nope // LANES
    groups = LANES // rope
    kv3 = kv.reshape(bsz, seq, -1)
    o = _flash(q.reshape(bsz, seq, -1), kv3, kv3[..., heads * nope:], qcol=0, kcol=0, vcol=0, n_blocks=nb,
               qx=q.reshape(bsz, seq, -1), qx_col=lambda h: nb + (2 * h) // groups, kx=kr.reshape(bsz, seq, LANES),
               extra_gw=rope)
    return o.reshape(bsz * seq, heads * vd)


def _moba_mixer(h16, bsz, seq, rot_p, w_qkv):
    width = MOBA_HEADS * HEAD_DIM
    wb = w_qkv.astype(BF16)
    qk, kmean = _proj(h16, wb[:, :2 * width], rot=rot_p, want_kmean=True, scale=HEAD_DIM ** -0.5 * LOG2E, scale_to=width)
    v = _proj(h16, wb[:, 2 * width:])
    n_kb = seq // MOBA_BLOCK
    km = kmean.reshape(bsz, n_kb, 2 * width)[:, :, width:].astype(BF16)
    km = jnp.pad(km, ((0, 0), (0, LANES - n_kb), (0, 0)))
    nb = width // LANES
    qk3 = qk.reshape(bsz, seq, -1)
    o = _flash(qk3, qk3, v.reshape(bsz, seq, -1), qcol=0, kcol=nb, vcol=0, n_blocks=nb, km=km)
    return o.reshape(bsz * seq, width)


def kernel(x, positions, diff_w_qkv, diff_lambda_q1, diff_lambda_k1, diff_lambda_q2, diff_lambda_k2, diff_subln_g,
           diff_w_o, fox_w_in, fox_b_f, fox_w_o, mla_w_down, mla_q_norm_g, mla_kv_norm_g, mla_w_uq, mla_w_ukv,
           mla_w_o, moba_w_qkv, moba_w_o, ffn_w_in, ffn_conv_w, ffn_conv_b, ffn_w_out, ln1_g, ln1_b, ln2_g, ln2_b):
    bsz, seq, d_model = x.shape
    depth = ffn_w_in.shape[0]
    alpha = (2 * depth) ** 0.25
    rot_p = _rot_tables(positions, PARTIAL_ROT, HEAD_DIM)
    rot_m = _rot_tables(positions, MLA_ROPE, MLA_ROPE)
    h32 = x.reshape(bsz * seq, d_model).astype(F32)
    h16 = h32.astype(BF16)
    for i in range(depth):
        m, u = i % N_MIXERS, i // N_MIXERS
        if m == 0:
            y = _diff_mixer(h16, bsz, seq, rot_p, diff_w_qkv[u], diff_lambda_q1[u], diff_lambda_k1[u],
                            diff_lambda_q2[u], diff_lambda_k2[u], diff_subln_g[u], i)
            w_o = diff_w_o[u]
        elif m == 1:
            y = _fox_mixer(h16, bsz, seq, fox_w_in[u], fox_b_f[u])
            w_o = fox_w_o[u]
        elif m == 2:
            y = _mla_mixer(h16, bsz, seq, rot_m, mla_w_down[u], mla_q_norm_g[u], mla_kv_norm_g[u], mla_w_uq[u],
                           mla_w_ukv[u])
            w_o = mla_w_o[u]
        else:
            y = _moba_mixer(h16, bsz, seq, rot_p, moba_w_qkv[u])
            w_o = moba_w_o[u]
        h32, h16 = _proj_ln(y, w_o.astype(BF16), h32, ln1_g[i].astype(F32), ln1_b[i].astype(F32), alpha=alpha)
        gu = _proj(h16, ffn_w_in[i].astype(BF16), out_dtype=F32)
        h32, h16 = _ffn_out(gu, ffn_conv_w[i].astype(F32), ffn_conv_b[i].astype(F32), ffn_w_out[i].astype(BF16),
                            h32, ln2_g[i].astype(F32), ln2_b[i].astype(F32), alpha=alpha, seq=seq)
    return h32.reshape(bsz, seq, d_model).astype(x.dtype)
```

```python
import functools
import math

import jax
import jax.numpy as jnp
import numpy as np
from jax import lax
from jax.experimental import pallas as pl
from jax.experimental.pallas import tpu as pltpu

F32 = jnp.float32
BF16 = jnp.bfloat16

LANES = 128
VMEM_LIMIT = 56 * 1024 * 1024

ROPE_THETA = 500000.0
LN_EPS = 1e-5
RMS_EPS = 1e-6
HEAD_DIM = 64
PARTIAL_ROT = HEAD_DIM // 4
DIFF_HEADS = 8
FOX_HEADS = 16
MLA_HEADS = 16
MLA_ROPE = 32
MLA_Q_RANK = 384
MLA_KV_RANK = 256
MOBA_HEADS = 16
MOBA_BLOCK = 256
MOBA_TOPK = 3
CONV_WIDTH = 3
N_MIXERS = 4

ONES_ROWS = 8
NEG = -1e30
LOG2E = math.log2(math.e)


def _params(*sem):
    return pltpu.CompilerParams(dimension_semantics=sem, vmem_limit_bytes=VMEM_LIMIT)


def _rot_tables(positions, rot_dim, period):
    half = rot_dim // 2
    inv_freq = ROPE_THETA ** (-jnp.arange(0, rot_dim, 2, dtype=F32) / rot_dim)
    ang = positions.astype(F32)[..., None] * inv_freq
    cos, sin = jnp.cos(ang), jnp.sin(ang)
    lane = np.arange(LANES)
    j = lane % period
    idx = j % half
    lo = j < half
    hi = (j >= half) & (j < rot_dim)
    cos_t = jnp.where(lo | hi, cos[..., idx], 1.0)
    sin_a = jnp.where(lo, -sin[..., idx], 0.0)
    sin_b = jnp.where(hi, sin[..., idx], 0.0)
    rows = positions.shape[0] * positions.shape[1]
    return tuple(t.reshape(rows, LANES) for t in (cos_t, sin_a, sin_b)), half


def _apply_rot(val, cos, sin_a, sin_b, half):
    pieces = []
    for c in range(val.shape[1] // LANES):
        blk = val[:, c * LANES:(c + 1) * LANES]
        pieces.append(blk * cos + pltpu.roll(blk, LANES - half, 1) * sin_a + pltpu.roll(blk, half, 1) * sin_b)
    return pieces[0] if len(pieces) == 1 else jnp.concatenate(pieces, axis=-1)


def _proj_kernel(*refs, rot_half, rot_tile0, want_kmean, tm, scale, scale_tiles):
    x_ref, w_ref = refs[:2]
    rest = refs[2:]
    if rot_half is not None:
        cos_ref, sa_ref, sb_ref = rest[:3]
        rest = rest[3:]
    o_ref = rest[0]
    km_ref = rest[1] if want_kmean else None
    acc = jnp.dot(x_ref[...], w_ref[...], preferred_element_type=F32)

    def emit(val):
        if want_kmean:
            tn = val.shape[1]
            km_ref[...] = jnp.mean(val.reshape(tm // MOBA_BLOCK, MOBA_BLOCK, tn), axis=1)
        if scale is not None:
            val = val * jnp.where(pl.program_id(1) < scale_tiles, scale, 1.0)
        o_ref[...] = val.astype(o_ref.dtype)

    if rot_half is None:
        emit(acc)
        return

    def rotated():
        return _apply_rot(acc, cos_ref[...], sa_ref[...], sb_ref[...], rot_half)

    if rot_tile0 == 0:
        emit(rotated())
    else:
        j = pl.program_id(1)

        @pl.when(j >= rot_tile0)
        def _():
            emit(rotated())

        @pl.when(j < rot_tile0)
        def _():
            emit(acc)


def _proj(x, w, *, out_dtype=BF16, rot=None, rot_from=0, want_kmean=False, scale=None, scale_to=0, tm=1024, tn=512):
    m, k = x.shape
    n = w.shape[1]
    tm, tn = min(tm, m), min(tn, n)
    assert m % tm == 0 and n % tn == 0 and rot_from % tn == 0 and scale_to % tn == 0
    in_specs = [pl.BlockSpec((tm, k), lambda i, j: (i, 0)), pl.BlockSpec((k, tn), lambda i, j: (0, j))]
    args = [x, w]
    rot_half = None
    if rot is not None:
        tabs, rot_half = rot
        in_specs += [pl.BlockSpec((tm, LANES), lambda i, j: (i, 0))] * 3
        args += list(tabs)
    out_shape = [jax.ShapeDtypeStruct((m, n), out_dtype)]
    out_specs = [pl.BlockSpec((tm, tn), lambda i, j: (i, j))]
    if want_kmean:
        assert tm % MOBA_BLOCK == 0
        out_shape.append(jax.ShapeDtypeStruct((m // tm, tm // MOBA_BLOCK, n), F32))
        out_specs.append(pl.BlockSpec((None, tm // MOBA_BLOCK, tn), lambda i, j: (i, 0, j)))
    res = pl.pallas_call(
        functools.partial(_proj_kernel, rot_half=rot_half, rot_tile0=rot_from // tn, want_kmean=want_kmean, tm=tm,
                          scale=scale, scale_tiles=scale_to // tn),
        grid=(m // tm, n // tn),
        in_specs=in_specs,
        out_specs=out_specs,
        out_shape=out_shape,
        compiler_params=_params("parallel", "arbitrary"),
        name="proj",
    )(*args)
    return res if want_kmean else res[0]


def _layer_norm_rows(z, g, b):
    mu = jnp.mean(z, axis=-1, keepdims=True)
    zc = z - mu
    var = jnp.mean(zc * zc, axis=-1, keepdims=True)
    return zc * lax.rsqrt(var + LN_EPS) * g + b


def _proj_ln_kernel(x_ref, w_ref, res_ref, g_ref, b_ref, h32_ref, h16_ref, *, alpha):
    y = jnp.dot(x_ref[...], w_ref[...], preferred_element_type=F32)
    h = _layer_norm_rows(alpha * res_ref[...] + y, g_ref[...], b_ref[...])
    h32_ref[...] = h
    h16_ref[...] = h.astype(BF16)


def _proj_ln(x, w, res, g, b, *, alpha, tm=512):
    m, k = x.shape
    n = w.shape[1]
    tm = min(tm, m)
    row = lambda i: (i, 0)
    const = lambda i: (0, 0)
    return pl.pallas_call(
        functools.partial(_proj_ln_kernel, alpha=alpha),
        grid=(m // tm,),
        in_specs=[pl.BlockSpec((tm, k), row), pl.BlockSpec((k, n), const), pl.BlockSpec((tm, n), row),
                  pl.BlockSpec((1, n), const), pl.BlockSpec((1, n), const)],
        out_specs=[pl.BlockSpec((tm, n), row), pl.BlockSpec((tm, n), row)],
        out_shape=[jax.ShapeDtypeStruct((m, n), F32), jax.ShapeDtypeStruct((m, n), BF16)],
        compiler_params=_params("parallel"),
        name="proj_ln",
    )(x, w, res, g.reshape(1, n), b.reshape(1, n))


HALO = 16


def _ffn_kernel(x_ref, halo_ref, win_ref, cw_ref, cb_ref, wout_ref, res_ref, g_ref, b_ref, h32_ref, h16_ref,
                *, alpha, tm, seq, chunk):
    i = pl.program_id(0)
    d_ff = wout_ref.shape[0]
    x = x_ref[...]
    halo = halo_ref[...]
    halo = jnp.where((i * tm) % seq == 0, jnp.zeros_like(halo), halo)
    xa = jnp.concatenate([halo, x], axis=0)
    acc = jnp.zeros((tm, wout_ref.shape[1]), F32)
    for c in range(d_ff // chunk):
        cs = slice(c * chunk, (c + 1) * chunk)
        gate = jnp.dot(xa, win_ref[:, cs], preferred_element_type=F32)
        up = jnp.dot(x, win_ref[:, d_ff + c * chunk:d_ff + (c + 1) * chunk], preferred_element_type=F32)
        g0 = gate[HALO:]
        g1 = pltpu.roll(gate, 1, 0)[HALO:]
        g2 = pltpu.roll(gate, 2, 0)[HALO:]
        cw = cw_ref[:, cs]
        conv = cw[0:1, :] * g2 + cw[1:2, :] * g1 + cw[2:3, :] * g0 + cb_ref[:, cs]
        act = conv * (1.0 / (1.0 + jnp.exp(-conv))) * up
        acc = acc + jnp.dot(act.astype(BF16), wout_ref[cs, :], preferred_element_type=F32)
    h = _layer_norm_rows(alpha * res_ref[...] + acc, g_ref[...], b_ref[...])
    h32_ref[...] = h
    h16_ref[...] = h.astype(BF16)


def _ffn(x16, w_in, conv_w, conv_b, w_out, res, g, b, *, alpha, seq, tm=512, chunk=256):
    m, d = x16.shape
    d_ff, n = w_out.shape
    tm = min(tm, m)
    assert seq % tm == 0 and tm % HALO == 0 and d_ff % chunk == 0 and w_in.shape == (d, 2 * d_ff)
    row = lambda i: (i, 0)
    const = lambda i: (0, 0)
    once = pl.Buffered(1)
    sub = tm // HALO
    return pl.pallas_call(
        functools.partial(_ffn_kernel, alpha=alpha, tm=tm, seq=seq, chunk=chunk),
        grid=(m // tm,),
        in_specs=[pl.BlockSpec((tm, d), row),
                  pl.BlockSpec((HALO, d), lambda i: (jnp.maximum(i * sub - 1, 0), 0)),
                  pl.BlockSpec((d, 2 * d_ff), const, pipeline_mode=once),
                  pl.BlockSpec((CONV_WIDTH, d_ff), const), pl.BlockSpec((1, d_ff), const),
                  pl.BlockSpec((d_ff, n), const, pipeline_mode=once), pl.BlockSpec((tm, n), row),
                  pl.BlockSpec((1, n), const), pl.BlockSpec((1, n), const)],
        out_specs=[pl.BlockSpec((tm, n), row), pl.BlockSpec((tm, n), row)],
        out_shape=[jax.ShapeDtypeStruct((m, n), F32), jax.ShapeDtypeStruct((m, n), BF16)],
        compiler_params=_params("parallel"),
        name="ffn",
    )(x16, x16, w_in, conv_w, conv_b.reshape(1, d_ff), w_out, res, g.reshape(1, n), b.reshape(1, n))


def _split3_bf16(v):
    hi = v.astype(BF16)
    r1 = v - hi.astype(F32)
    mid = r1.astype(BF16)
    lo = (r1 - mid.astype(F32)).astype(BF16)
    return hi, mid, lo


def _fox_gate_kernel(x_ref, w_ref, bf_ref, eq_ref, ek_ref, oq_ref, ok_ref, qx_ref, kx_ref, carry_ref, *, tm, inv_scale):
    @pl.when(pl.program_id(1) == 0)
    def _():
        carry_ref[...] = jnp.zeros_like(carry_ref)

    z = jnp.dot(x_ref[...], w_ref[...], preferred_element_type=F32) + bf_ref[...]
    log_f = jnp.minimum(z, 0.0) - jnp.log(1.0 + jnp.exp(-jnp.abs(z)))
    r = lax.broadcasted_iota(jnp.int32, (tm, tm), 0)
    c = lax.broadcasted_iota(jnp.int32, (tm, tm), 1)
    tri = (c <= r).astype(F32)
    csum = jnp.dot(tri, log_f, preferred_element_type=F32, precision=lax.Precision.HIGHEST) + carry_ref[...]
    carry_ref[...] = csum[tm - 1:tm, :]
    parts = jnp.concatenate(_split3_bf16(csum * inv_scale), axis=-1)
    qx_ref[...] = (jnp.dot(parts, eq_ref[...], preferred_element_type=F32) + oq_ref[...]).astype(BF16)
    kx_ref[...] = (jnp.dot(parts, ek_ref[...], preferred_element_type=F32) + ok_ref[...]).astype(BF16)


def _fox_bias_layout():
    gw = LANES // FOX_HEADS
    eq = np.zeros((3 * LANES, LANES), np.float32)
    ek = np.zeros((3 * LANES, LANES), np.float32)
    oq = np.zeros((1, LANES), np.float32)
    ok = np.zeros((1, LANES), np.float32)
    for h in range(FOX_HEADS):
        for t in range(3):
            eq[t * LANES + h, gw * h + t] = 1.0
            ek[t * LANES + h, gw * h + 3 + t] = -1.0
            oq[0, gw * h + 3 + t] = 1.0
            ok[0, gw * h + t] = 1.0
    return jnp.asarray(eq, BF16), jnp.asarray(ek, BF16), jnp.asarray(oq), jnp.asarray(ok), gw


def _fox_gate(x3, w_f, b_f, *, inv_scale, tm=256):
    bsz, seq, k = x3.shape
    tm = min(tm, seq)
    eq, ek, oq, ok, gw = _fox_bias_layout()
    const = lambda b, i: (0, 0)
    out_spec = pl.BlockSpec((None, tm, LANES), lambda b, i: (b, i, 0))
    qx, kx = pl.pallas_call(
        functools.partial(_fox_gate_kernel, tm=tm, inv_scale=inv_scale),
        grid=(bsz, seq // tm),
        in_specs=[pl.BlockSpec((None, tm, k), lambda b, i: (b, i, 0)),
                  pl.BlockSpec((k, LANES), const), pl.BlockSpec((1, LANES), const),
                  pl.BlockSpec((3 * LANES, LANES), const), pl.BlockSpec((3 * LANES, LANES), const),
                  pl.BlockSpec((1, LANES), const), pl.BlockSpec((1, LANES), const)],
        out_specs=[out_spec, out_spec],
        out_shape=[jax.ShapeDtypeStruct((bsz, seq, LANES), BF16)] * 2,
        scratch_shapes=[pltpu.VMEM((1, LANES), F32)],
        compiler_params=_params("parallel", "arbitrary"),
        name="fox_gate",
    )(x3, w_f, b_f, eq, ek, oq, ok)
    return qx, kx, gw


def _rms_rows(v, g):
    return v * lax.rsqrt(jnp.mean(v * v, axis=-1, keepdims=True) + RMS_EPS) * g


def _mla_down_kernel(x_ref, w_ref, gq_ref, gkv_ref, cos_ref, sa_ref, sb_ref, cq_ref, ckv_ref, kr_ref, *, half):
    acc = jnp.dot(x_ref[...], w_ref[...], preferred_element_type=F32)
    q_end, kv_end = MLA_Q_RANK, MLA_Q_RANK + MLA_KV_RANK
    cq_ref[...] = _rms_rows(acc[:, :q_end], gq_ref[...]).astype(BF16)
    ckv_ref[...] = _rms_rows(acc[:, q_end:kv_end], gkv_ref[...]).astype(BF16)
    kr_ref[...] = _apply_rot(acc[:, kv_end:], cos_ref[...], sa_ref[...], sb_ref[...], half).astype(BF16)


def _mla_down(x, w, gq, gkv, rot, *, tm=1024):
    m, k = x.shape
    n = w.shape[1]
    tm = min(tm, m)
    tabs, half = rot
    row = lambda i: (i, 0)
    const = lambda i: (0, 0)
    return pl.pallas_call(
        functools.partial(_mla_down_kernel, half=half),
        grid=(m // tm,),
        in_specs=[pl.BlockSpec((tm, k), row), pl.BlockSpec((k, n), const),
                  pl.BlockSpec((1, MLA_Q_RANK), const), pl.BlockSpec((1, MLA_KV_RANK), const)]
                 + [pl.BlockSpec((tm, LANES), row)] * 3,
        out_specs=[pl.BlockSpec((tm, MLA_Q_RANK), row), pl.BlockSpec((tm, MLA_KV_RANK), row),
                   pl.BlockSpec((tm, LANES), row)],
        out_shape=[jax.ShapeDtypeStruct((m, MLA_Q_RANK), BF16), jax.ShapeDtypeStruct((m, MLA_KV_RANK), BF16),
                   jax.ShapeDtypeStruct((m, LANES), BF16)],
        compiler_params=_params("parallel"),
        name="mla_down",
    )(x, w, gq.reshape(1, -1), gkv.reshape(1, -1), *tabs)


def _flash_kernel(*refs, tq, tk, mode, extra_gw, heads_per_block, moba, lam_init):
    it = iter(refs)
    q_ref, k_ref, vt_ref = next(it), next(it), next(it)
    qx_ref = kx_ref = km_ref = lamp_ref = subg_ref = g_scr = None
    if extra_gw is not None:
        qx_ref, kx_ref = next(it), next(it)
    if moba:
        km_ref = next(it)
    if mode == "diff":
        lamp_ref, subg_ref = next(it), next(it)
    o_ref = next(it)
    m_scr, acc_scr, s_scr = next(it), next(it), next(it)
    if moba:
        g_scr = next(it)

    hb = pl.program_id(1)
    qi = pl.program_id(2)
    lane = lax.broadcasted_iota(jnp.int32, (1, LANES), 1)
    q = q_ref[...]
    zero = jnp.zeros_like(q)
    q_half = [jnp.where(lane // HEAD_DIM == e, q, zero) for e in range(2)]

    if extra_gw is not None:
        groups = LANES // extra_gw
        qx = qx_ref[...]
        q_aug = []
        for e in range(2):
            grp = (hb * heads_per_block + e) % groups
            q_aug.append(jnp.concatenate([q_half[e], jnp.where(lane // extra_gw == grp, qx, jnp.zeros_like(qx))], -1))
    elif moba:
        q_aug = _moba_bias(q_half, km_ref, g_scr, qi, tq, moba)
    else:
        q_aug = q_half

    m_scr[...] = jnp.full_like(m_scr, NEG)
    acc_scr[...] = jnp.zeros_like(acc_scr)
    row = lax.broadcasted_iota(jnp.int32, (LANES, 1), 0)

    def scores(slot, j):
        start = pl.multiple_of(j * tk, tk)
        k = k_ref[pl.ds(start, tk), :]
        if extra_gw is not None:
            k = jnp.concatenate([k, kx_ref[pl.ds(start, tk), :]], -1)
        elif moba:
            blk = (start + lax.broadcasted_iota(jnp.int32, (tk, 1), 0)) // MOBA_BLOCK
            k = jnp.concatenate([k, jnp.where(blk == lane, 1.0, 0.0).astype(BF16)], -1)
        for e in range(2):
            s_scr[slot, e] = lax.dot_general(k, q_aug[e], (((1,), (1,)), ((), ())), preferred_element_type=F32)

    def consume(slot, j, masked):
        vt = vt_ref[j]
        one = jnp.ones_like(vt)
        if mode == "pair":
            vt_e = [jnp.where(row // HEAD_DIM == e, vt, one) for e in range(2)]
        else:
            vt_e = [jnp.concatenate([vt, one[:ONES_ROWS]], 0)] * 2
        for e in range(2):
            se = s_scr[slot, e]
            if masked:
                key = j * tk + lax.broadcasted_iota(jnp.int32, (tk, tq), 0)
                qry = qi * tq + lax.broadcasted_iota(jnp.int32, (tk, tq), 1)
                se = jnp.where(key <= qry, se, NEG)
            m_old = m_scr[e]
            m_new = jnp.maximum(m_old, jnp.max(se, axis=0, keepdims=True))
            p = jnp.exp2(se - m_new)
            alpha = jnp.exp2(m_old - m_new)
            acc_scr[e] = alpha * acc_scr[e] + jnp.dot(vt_e[e], p.astype(BF16), preferred_element_type=F32)
            m_scr[e] = m_new

    n_full = (qi * tq) // tk
    n_diag = max(tq // tk, 1)
    scores(0, 0)

    def body(t, carry):
        j = 2 * t
        scores(1, j + 1)
        consume(0, j, False)
        scores(0, j + 2)
        consume(1, j + 1, False)
        return carry

    lax.fori_loop(0, n_full // 2, body, 0)

    def drain(first, n_unmasked):
        total = n_unmasked + n_diag
        for i in range(total):
            if i + 1 < total:
                scores((i + 1) % 2, first + i + 1)
            consume(i % 2, first + i, i >= n_unmasked)

    if (tq // tk) % 2 == 0 and tq >= tk:
        drain(n_full, 0)
    else:
        @pl.when(n_full % 2 == 1)
        def _():
            drain(n_full - 1, 1)

        @pl.when(n_full % 2 == 0)
        def _():
            drain(n_full, 0)

    if mode == "pair":
        a0, a1 = acc_scr[0], acc_scr[1]
        o_t = jnp.concatenate([a0[:HEAD_DIM] / a0[HEAD_DIM:HEAD_DIM + 1], a1[HEAD_DIM:] / a1[0:1]], axis=0)
        o_ref[...] = o_t.T.astype(o_ref.dtype)
    else:
        o0 = (acc_scr[0, :LANES, :] / acc_scr[0, LANES:LANES + 1, :]).T
        o1 = (acc_scr[1, :LANES, :] / acc_scr[1, LANES:LANES + 1, :]).T
        lp = lamp_ref[...]
        lam = (jnp.exp(jnp.sum(lp[0:1] * lp[1:2], axis=-1, keepdims=True))
               - jnp.exp(jnp.sum(lp[2:3] * lp[3:4], axis=-1, keepdims=True)) + lam_init)
        o = o0 - lam * o1
        o = o * lax.rsqrt(jnp.mean(o * o, axis=-1, keepdims=True) + RMS_EPS) * subg_ref[...] * (1.0 - lam_init)
        o_ref[...] = o.astype(o_ref.dtype)


def _moba_bias(q_half, km_ref, g_scr, qi, tq, n_blocks):
    rows = max(8, -(-n_blocks // 8) * 8)
    km = km_ref[...]
    blk = lax.broadcasted_iota(jnp.int32, (LANES, 1), 0)
    cur = (qi * tq + lax.broadcasted_iota(jnp.int32, (1, tq), 1)) // MOBA_BLOCK
    for e in range(2):
        gate = lax.dot_general(km, q_half[e], (((1,), (1,)), ((), ())), preferred_element_type=F32)
        g_scr[e] = jnp.where(blk < cur, gate, NEG)
    blk_r = blk[:rows]

    def count(n, ranks):
        out = []
        for e in range(2):
            gn = g_scr[e, pl.ds(n, 1), :]
            g = g_scr[e, :rows, :]
            beats = (gn > g) | ((gn == g) & (n < blk_r))
            out.append(ranks[e] + jnp.where(beats, 1.0, 0.0))
        return tuple(out)

    n_past = (qi * tq + tq - 1) // MOBA_BLOCK
    ranks = lax.fori_loop(0, n_past, count, (jnp.zeros((rows, tq), F32),) * 2)
    q_aug = []
    for e in range(2):
        keep = ((blk_r < cur) & (ranks[e] < MOBA_TOPK)) | (blk_r == cur)
        bias_t = jnp.where(keep, 0.0, NEG)
        if rows < LANES:
            bias_t = jnp.concatenate([bias_t, jnp.zeros((LANES - rows, tq), F32)], axis=0)
        q_aug.append(jnp.concatenate([q_half[e], bias_t.T.astype(BF16)], -1))
    return q_aug


def _key_major(v, tk):
    bsz, seq, c = v.shape
    return v.reshape(bsz, seq // tk, tk, c).transpose(0, 1, 3, 2)


def _flash(q, k, vt, *, qcol, kcol, vcol, n_blocks, mode="pair", qx=None, qx_col=None, kx=None, extra_gw=None,
           km=None, lam_params=None, subg=None, lam_init=None, tq=1024, tk=1024):
    bsz, seq, _ = q.shape
    tq, tk = min(tq, seq), min(tk, seq)
    assert seq % tq == 0 and seq % tk == 0 and (tk % tq == 0 or tq % tk == 0)
    heads_per_block = 2 if mode == "pair" else 1
    in_specs = [pl.BlockSpec((None, tq, LANES), lambda b, h, i: (b, i, qcol + h)),
                pl.BlockSpec((None, seq, LANES), lambda b, h, i: (b, 0, kcol + h)),
                pl.BlockSpec((None, seq // tk, LANES, tk), lambda b, h, i: (b, 0, vcol + h, 0))]
    args = [q, k, _key_major(vt, tk)]
    acc_rows = LANES if mode == "pair" else LANES + ONES_ROWS
    scratch = [pltpu.VMEM((2, 1, tq), F32), pltpu.VMEM((2, acc_rows, tq), F32), pltpu.VMEM((2, 2, tk, tq), F32)]
    if extra_gw is not None:
        in_specs += [pl.BlockSpec((None, tq, LANES), lambda b, h, i: (b, i, qx_col(h))),
                     pl.BlockSpec((None, seq, LANES), lambda b, h, i: (b, 0, 0))]
        args += [qx, kx]
    moba = 0
    if km is not None:
        moba = seq // MOBA_BLOCK
        assert moba <= LANES and seq % MOBA_BLOCK == 0 and min(MOBA_TOPK, max(moba - 1, 1)) == MOBA_TOPK
        in_specs.append(pl.BlockSpec((None, LANES, LANES), lambda b, h, i: (b, 0, h)))
        args.append(km)
        scratch.append(pltpu.VMEM((2, LANES, tq), F32))
    if mode == "diff":
        in_specs += [pl.BlockSpec((4, HEAD_DIM), lambda b, h, i: (0, 0)), pl.BlockSpec((1, LANES), lambda b, h, i: (0, 0))]
        args += [lam_params, subg]
    return pl.pallas_call(
        functools.partial(_flash_kernel, tq=tq, tk=tk, mode=mode, extra_gw=extra_gw,
                          heads_per_block=heads_per_block, moba=moba, lam_init=lam_init),
        grid=(bsz, n_blocks, seq // tq),
        in_specs=in_specs,
        out_specs=pl.BlockSpec((None, tq, LANES), lambda b, h, i: (b, i, h)),
        out_shape=jax.ShapeDtypeStruct((bsz, seq, n_blocks * LANES), BF16),
        scratch_shapes=scratch,
        compiler_params=_params("parallel", "parallel", "arbitrary"),
        name="flash_" + mode + ("_x" if extra_gw else "") + ("_moba" if moba else ""),
    )(*args)


def _diff_lambda_init(layer):
    return 0.8 - 0.6 * math.exp(-0.3 * layer)


def _diff_mixer(h16, bsz, seq, rot_p, w_qkv, lq1, lk1, lq2, lk2, subln_g, layer):
    width = DIFF_HEADS * 2 * HEAD_DIM
    wb = w_qkv.astype(BF16)
    qk = _proj(h16, wb[:, :2 * width], rot=rot_p, scale=HEAD_DIM ** -0.5 * LOG2E, scale_to=width)
    v = _proj(h16, wb[:, 2 * width:])
    nb = width // LANES
    lam_params = jnp.stack([lq1, lk1, lq2, lk2]).astype(F32)
    o = _flash(qk.reshape(bsz, seq, -1), qk.reshape(bsz, seq, -1), v.reshape(bsz, seq, -1),
               qcol=0, kcol=nb, vcol=0, n_blocks=nb, mode="diff",
               lam_params=lam_params, subg=subln_g.reshape(1, LANES).astype(F32), lam_init=_diff_lambda_init(layer))
    return o.reshape(bsz * seq, width)


def _fox_mixer(h16, bsz, seq, w_in, b_f):
    width = FOX_HEADS * HEAD_DIM
    scale = HEAD_DIM ** -0.5
    wb = w_in.astype(BF16)
    qkv = _proj(h16, wb[:, :3 * width], scale=scale * LOG2E, scale_to=width)
    w_f = jnp.pad(wb[:, 3 * width:], ((0, 0), (0, LANES - FOX_HEADS)))
    bf = jnp.pad(b_f.astype(F32), (0, LANES - FOX_HEADS)).reshape(1, LANES)
    qx, kx, gw = _fox_gate(h16.reshape(bsz, seq, -1), w_f, bf, inv_scale=LOG2E)
    nb = width // LANES
    qkv3 = qkv.reshape(bsz, seq, -1)
    o = _flash(qkv3, qkv3, qkv3[..., 2 * width:], qcol=0, kcol=nb, vcol=0, n_blocks=nb,
               qx=qx, qx_col=lambda h: 0, kx=kx, extra_gw=gw)
    return o.reshape(bsz * seq, width)


def _mla_mixer(h16, bsz, seq, rot_m, w_down, gq, gkv, w_uq, w_ukv):
    nope, rope, vd, heads = HEAD_DIM, MLA_ROPE, HEAD_DIM, MLA_HEADS
    wd = w_down.astype(BF16)
    kv_end = MLA_Q_RANK + MLA_KV_RANK
    wd = jnp.concatenate([wd[:, :kv_end]] + [wd[:, kv_end:]] * (LANES // rope), axis=1)
    cq, ckv, kr = _mla_down(h16, wd, gq.astype(F32), gkv.astype(F32), rot_m)
    wq = w_uq.astype(BF16).reshape(MLA_Q_RANK, heads, nope + rope)
    wq = jnp.concatenate([wq[:, :, :nope].reshape(MLA_Q_RANK, -1), wq[:, :, nope:].reshape(MLA_Q_RANK, -1)], axis=1)
    wkv = w_ukv.astype(BF16).reshape(MLA_KV_RANK, heads, nope + vd)
    wkv = jnp.concatenate([wkv[:, :, :nope].reshape(MLA_KV_RANK, -1), wkv[:, :, nope:].reshape(MLA_KV_RANK, -1)], axis=1)
    q = _proj(cq, wq, rot=rot_m, rot_from=heads * nope, scale=(nope + rope) ** -0.5 * LOG2E, scale_to=wq.shape[1])
    kv = _proj(ckv, wkv)
    nb = heads * nope // LANES
    groups = LANES // rope
    kv3 = kv.reshape(bsz, seq, -1)
    o = _flash(q.reshape(bsz, seq, -1), kv3, kv3[..., heads * nope:], qcol=0, kcol=0, vcol=0, n_blocks=nb,
               qx=q.reshape(bsz, seq, -1), qx_col=lambda h: nb + (2 * h) // groups, kx=kr.reshape(bsz, seq, LANES),
               extra_gw=rope)
    return o.reshape(bsz * seq, heads * vd)


def _moba_mixer(h16, bsz, seq, rot_p, w_qkv):
    width = MOBA_HEADS * HEAD_DIM
    wb = w_qkv.astype(BF16)
    qk, kmean = _proj(h16, wb[:, :2 * width], rot=rot_p, want_kmean=True, scale=HEAD_DIM ** -0.5 * LOG2E, scale_to=width)
    v = _proj(h16, wb[:, 2 * width:])
    n_kb = seq // MOBA_BLOCK
    km = kmean.reshape(bsz, n_kb, 2 * width)[:, :, width:].astype(BF16)
    km = jnp.pad(km, ((0, 0), (0, LANES - n_kb), (0, 0)))
    nb = width // LANES
    qk3 = qk.reshape(bsz, seq, -1)
    o = _flash(qk3, qk3, v.reshape(bsz, seq, -1), qcol=0, kcol=nb, vcol=0, n_blocks=nb, km=km)
    return o.reshape(bsz * seq, width)


def kernel(x, positions, diff_w_qkv, diff_lambda_q1, diff_lambda_k1, diff_lambda_q2, diff_lambda_k2, diff_subln_g,
           diff_w_o, fox_w_in, fox_b_f, fox_w_o, mla_w_down, mla_q_norm_g, mla_kv_norm_g, mla_w_uq, mla_w_ukv,
           mla_w_o, moba_w_qkv, moba_w_o, ffn_w_in, ffn_conv_w, ffn_conv_b, ffn_w_out, ln1_g, ln1_b, ln2_g, ln2_b):
    bsz, seq, d_model = x.shape
    depth = ffn_w_in.shape[0]
    alpha = (2 * depth) ** 0.25
    rot_p = _rot_tables(positions, PARTIAL_ROT, HEAD_DIM)
    rot_m = _rot_tables(positions, MLA_ROPE, MLA_ROPE)
    h32 = x.reshape(bsz * seq, d_model).astype(F32)
    h16 = h32.astype(BF16)
    for i in range(depth):
        m, u = i % N_MIXERS, i // N_MIXERS
        if m == 0:
            y = _diff_mixer(h16, bsz, seq, rot_p, diff_w_qkv[u], diff_lambda_q1[u], diff_lambda_k1[u],
                            diff_lambda_q2[u], diff_lambda_k2[u], diff_subln_g[u], i)
            w_o = diff_w_o[u]
        elif m == 1:
            y = _fox_mixer(h16, bsz, seq, fox_w_in[u], fox_b_f[u])
            w_o = fox_w_o[u]
        elif m == 2:
            y = _mla_mixer(h16, bsz, seq, rot_m, mla_w_down[u], mla_q_norm_g[u], mla_kv_norm_g[u], mla_w_uq[u],
                           mla_w_ukv[u])
            w_o = mla_w_o[u]
        else:
            y = _moba_mixer(h16, bsz, seq, rot_p, moba_w_qkv[u])
            w_o = moba_w_o[u]
        h32, h16 = _proj_ln(y, w_o.astype(BF16), h32, ln1_g[i].astype(F32), ln1_b[i].astype(F32), alpha=alpha)
        h32, h16 = _ffn(h16, ffn_w_in[i].astype(BF16), ffn_conv_w[i].astype(F32), ffn_conv_b[i].astype(F32),
                        ffn_w_out[i].astype(BF16), h32, ln2_g[i].astype(F32), ln2_b[i].astype(F32), alpha=alpha, seq=seq)
    return h32.reshape(bsz, seq, d_model).astype(x.dtype)
```

```python
import functools
import math

import jax
import jax.numpy as jnp
import numpy as np
from jax import lax
from jax.experimental import pallas as pl
from jax.experimental.pallas import tpu as pltpu

F32 = jnp.float32
BF16 = jnp.bfloat16

LANES = 128
VMEM_LIMIT = 56 * 1024 * 1024

ROPE_THETA = 500000.0
LN_EPS = 1e-5
RMS_EPS = 1e-6
HEAD_DIM = 64
PARTIAL_ROT = HEAD_DIM // 4
DIFF_HEADS = 8
FOX_HEADS = 16
MLA_HEADS = 16
MLA_ROPE = 32
MLA_Q_RANK = 384
MLA_KV_RANK = 256
MOBA_HEADS = 16
MOBA_BLOCK = 256
MOBA_TOPK = 3
CONV_WIDTH = 3
N_MIXERS = 4

QCHUNK = 256
ONES_ROWS = 8
NEG = -1e30
LOG2E = math.log2(math.e)


def _params(*sem):
    return pltpu.CompilerParams(dimension_semantics=sem, vmem_limit_bytes=VMEM_LIMIT)


def _rot_tables(positions, rot_dim, period):
    half = rot_dim // 2
    inv_freq = ROPE_THETA ** (-jnp.arange(0, rot_dim, 2, dtype=F32) / rot_dim)
    ang = positions.astype(F32)[..., None] * inv_freq
    cos, sin = jnp.cos(ang), jnp.sin(ang)
    lane = np.arange(LANES)
    j = lane % period
    idx = j % half
    lo = j < half
    hi = (j >= half) & (j < rot_dim)
    cos_t = jnp.where(lo | hi, cos[..., idx], 1.0)
    sin_a = jnp.where(lo, -sin[..., idx], 0.0)
    sin_b = jnp.where(hi, sin[..., idx], 0.0)
    rows = positions.shape[0] * positions.shape[1]
    return tuple(t.reshape(rows, LANES) for t in (cos_t, sin_a, sin_b)), half


def _apply_rot(val, cos, sin_a, sin_b, half):
    pieces = []
    for c in range(val.shape[1] // LANES):
        blk = val[:, c * LANES:(c + 1) * LANES]
        pieces.append(blk * cos + pltpu.roll(blk, LANES - half, 1) * sin_a + pltpu.roll(blk, half, 1) * sin_b)
    return pieces[0] if len(pieces) == 1 else jnp.concatenate(pieces, axis=-1)


def _proj_kernel(*refs, rot_half, rot_tile0, want_kmean, tm, scale, scale_tiles):
    x_ref, w_ref = refs[:2]
    rest = refs[2:]
    if rot_half is not None:
        cos_ref, sa_ref, sb_ref = rest[:3]
        rest = rest[3:]
    o_ref = rest[0]
    km_ref = rest[1] if want_kmean else None
    acc = jnp.dot(x_ref[...], w_ref[...], preferred_element_type=F32)

    def emit(val):
        if want_kmean:
            tn = val.shape[1]
            km_ref[...] = jnp.mean(val.reshape(tm // MOBA_BLOCK, MOBA_BLOCK, tn), axis=1)
        if scale is not None:
            val = val * jnp.where(pl.program_id(1) < scale_tiles, scale, 1.0)
        o_ref[...] = val.astype(o_ref.dtype)

    if rot_half is None:
        emit(acc)
        return

    def rotated():
        return _apply_rot(acc, cos_ref[...], sa_ref[...], sb_ref[...], rot_half)

    if rot_tile0 == 0:
        emit(rotated())
    else:
        j = pl.program_id(1)

        @pl.when(j >= rot_tile0)
        def _():
            emit(rotated())

        @pl.when(j < rot_tile0)
        def _():
            emit(acc)


def _proj(x, w, *, out_dtype=BF16, rot=None, rot_from=0, want_kmean=False, scale=None, scale_to=0, tm=1024, tn=512):
    m, k = x.shape
    n = w.shape[1]
    tm, tn = min(tm, m), min(tn, n)
    assert m % tm == 0 and n % tn == 0 and rot_from % tn == 0 and scale_to % tn == 0
    in_specs = [pl.BlockSpec((tm, k), lambda i, j: (i, 0)), pl.BlockSpec((k, tn), lambda i, j: (0, j))]
    args = [x, w]
    rot_half = None
    if rot is not None:
        tabs, rot_half = rot
        in_specs += [pl.BlockSpec((tm, LANES), lambda i, j: (i, 0))] * 3
        args += list(tabs)
    out_shape = [jax.ShapeDtypeStruct((m, n), out_dtype)]
    out_specs = [pl.BlockSpec((tm, tn), lambda i, j: (i, j))]
    if want_kmean:
        assert tm % MOBA_BLOCK == 0
        out_shape.append(jax.ShapeDtypeStruct((m // tm, tm // MOBA_BLOCK, n), F32))
        out_specs.append(pl.BlockSpec((None, tm // MOBA_BLOCK, tn), lambda i, j: (i, 0, j)))
    res = pl.pallas_call(
        functools.partial(_proj_kernel, rot_half=rot_half, rot_tile0=rot_from // tn, want_kmean=want_kmean, tm=tm,
                          scale=scale, scale_tiles=scale_to // tn),
        grid=(m // tm, n // tn),
        in_specs=in_specs,
        out_specs=out_specs,
        out_shape=out_shape,
        compiler_params=_params("parallel", "arbitrary"),
        name="proj",
    )(*args)
    return res if want_kmean else res[0]


def _layer_norm_rows(z, g, b):
    mu = jnp.mean(z, axis=-1, keepdims=True)
    zc = z - mu
    var = jnp.mean(zc * zc, axis=-1, keepdims=True)
    return zc * lax.rsqrt(var + LN_EPS) * g + b


def _proj_ln_kernel(x_ref, w_ref, res_ref, g_ref, b_ref, h32_ref, h16_ref, *, alpha):
    y = jnp.dot(x_ref[...], w_ref[...], preferred_element_type=F32)
    h = _layer_norm_rows(alpha * res_ref[...] + y, g_ref[...], b_ref[...])
    h32_ref[...] = h
    h16_ref[...] = h.astype(BF16)


def _proj_ln(x, w, res, g, b, *, alpha, tm=512):
    m, k = x.shape
    n = w.shape[1]
    tm = min(tm, m)
    row = lambda i: (i, 0)
    const = lambda i: (0, 0)
    return pl.pallas_call(
        functools.partial(_proj_ln_kernel, alpha=alpha),
        grid=(m // tm,),
        in_specs=[pl.BlockSpec((tm, k), row), pl.BlockSpec((k, n), const), pl.BlockSpec((tm, n), row),
                  pl.BlockSpec((1, n), const), pl.BlockSpec((1, n), const)],
        out_specs=[pl.BlockSpec((tm, n), row), pl.BlockSpec((tm, n), row)],
        out_shape=[jax.ShapeDtypeStruct((m, n), F32), jax.ShapeDtypeStruct((m, n), BF16)],
        compiler_params=_params("parallel"),
        name="proj_ln",
    )(x, w, res, g.reshape(1, n), b.reshape(1, n))


HALO = 16


def _ffn_kernel(x_ref, halo_ref, win_ref, cw_ref, cb_ref, wout_ref, res_ref, g_ref, b_ref, h32_ref, h16_ref,
                *, alpha, tm, seq, chunk):
    i = pl.program_id(0)
    d_ff = wout_ref.shape[0]
    x = x_ref[...]
    halo = halo_ref[...]
    halo = jnp.where((i * tm) % seq == 0, jnp.zeros_like(halo), halo)
    xa = jnp.concatenate([halo, x], axis=0)
    def up_proj(c):
        cs = slice(c * chunk, (c + 1) * chunk)
        gate = jnp.dot(xa, win_ref[:, cs], preferred_element_type=F32)
        up = jnp.dot(x, win_ref[:, d_ff + c * chunk:d_ff + (c + 1) * chunk], preferred_element_type=F32)
        return gate, up

    acc = jnp.zeros((tm, wout_ref.shape[1]), F32)
    n_c = d_ff // chunk
    nxt = up_proj(0)
    for c in range(n_c):
        cs = slice(c * chunk, (c + 1) * chunk)
        gate, up = nxt
        if c + 1 < n_c:
            nxt = up_proj(c + 1)
        g0 = gate[HALO:]
        g1 = pltpu.roll(gate, 1, 0)[HALO:]
        g2 = pltpu.roll(gate, 2, 0)[HALO:]
        cw = cw_ref[:, cs]
        conv = cw[0:1, :] * g2 + cw[1:2, :] * g1 + cw[2:3, :] * g0 + cb_ref[:, cs]
        act = conv * (1.0 / (1.0 + jnp.exp(-conv))) * up
        acc = acc + jnp.dot(act.astype(BF16), wout_ref[cs, :], preferred_element_type=F32)
    h = _layer_norm_rows(alpha * res_ref[...] + acc, g_ref[...], b_ref[...])
    h32_ref[...] = h
    h16_ref[...] = h.astype(BF16)


def _ffn(x16, w_in, conv_w, conv_b, w_out, res, g, b, *, alpha, seq, tm=512, chunk=256):
    m, d = x16.shape
    d_ff, n = w_out.shape
    tm = min(tm, m)
    assert seq % tm == 0 and tm % HALO == 0 and d_ff % chunk == 0 and w_in.shape == (d, 2 * d_ff)
    row = lambda i: (i, 0)
    const = lambda i: (0, 0)
    once = pl.Buffered(1)
    sub = tm // HALO
    return pl.pallas_call(
        functools.partial(_ffn_kernel, alpha=alpha, tm=tm, seq=seq, chunk=chunk),
        grid=(m // tm,),
        in_specs=[pl.BlockSpec((tm, d), row),
                  pl.BlockSpec((HALO, d), lambda i: (jnp.maximum(i * sub - 1, 0), 0)),
                  pl.BlockSpec((d, 2 * d_ff), const, pipeline_mode=once),
                  pl.BlockSpec((CONV_WIDTH, d_ff), const), pl.BlockSpec((1, d_ff), const),
                  pl.BlockSpec((d_ff, n), const, pipeline_mode=once), pl.BlockSpec((tm, n), row),
                  pl.BlockSpec((1, n), const), pl.BlockSpec((1, n), const)],
        out_specs=[pl.BlockSpec((tm, n), row), pl.BlockSpec((tm, n), row)],
        out_shape=[jax.ShapeDtypeStruct((m, n), F32), jax.ShapeDtypeStruct((m, n), BF16)],
        compiler_params=_params("parallel"),
        name="ffn",
    )(x16, x16, w_in, conv_w, conv_b.reshape(1, d_ff), w_out, res, g.reshape(1, n), b.reshape(1, n))


def _split3_bf16(v):
    hi = v.astype(BF16)
    r1 = v - hi.astype(F32)
    mid = r1.astype(BF16)
    lo = (r1 - mid.astype(F32)).astype(BF16)
    return hi, mid, lo


def _fox_gate_kernel(x_ref, w_ref, bf_ref, eq_ref, ek_ref, oq_ref, ok_ref, qx_ref, kx_ref, carry_ref, *, tm, inv_scale):
    @pl.when(pl.program_id(1) == 0)
    def _():
        carry_ref[...] = jnp.zeros_like(carry_ref)

    z = jnp.dot(x_ref[...], w_ref[...], preferred_element_type=F32) + bf_ref[...]
    log_f = jnp.minimum(z, 0.0) - jnp.log(1.0 + jnp.exp(-jnp.abs(z)))
    r = lax.broadcasted_iota(jnp.int32, (tm, tm), 0)
    c = lax.broadcasted_iota(jnp.int32, (tm, tm), 1)
    tri = (c <= r).astype(F32)
    csum = jnp.dot(tri, log_f, preferred_element_type=F32, precision=lax.Precision.HIGHEST) + carry_ref[...]
    carry_ref[...] = csum[tm - 1:tm, :]
    parts = jnp.concatenate(_split3_bf16(csum * inv_scale), axis=-1)
    qx_ref[...] = (jnp.dot(parts, eq_ref[...], preferred_element_type=F32) + oq_ref[...]).astype(BF16)
    kx_ref[...] = (jnp.dot(parts, ek_ref[...], preferred_element_type=F32) + ok_ref[...]).astype(BF16)


def _fox_bias_layout():
    gw = LANES // FOX_HEADS
    eq = np.zeros((3 * LANES, LANES), np.float32)
    ek = np.zeros((3 * LANES, LANES), np.float32)
    oq = np.zeros((1, LANES), np.float32)
    ok = np.zeros((1, LANES), np.float32)
    for h in range(FOX_HEADS):
        for t in range(3):
            eq[t * LANES + h, gw * h + t] = 1.0
            ek[t * LANES + h, gw * h + 3 + t] = -1.0
            oq[0, gw * h + 3 + t] = 1.0
            ok[0, gw * h + t] = 1.0
    return jnp.asarray(eq, BF16), jnp.asarray(ek, BF16), jnp.asarray(oq), jnp.asarray(ok), gw


def _fox_gate(x3, w_f, b_f, *, inv_scale, tm=256):
    bsz, seq, k = x3.shape
    tm = min(tm, seq)
    eq, ek, oq, ok, gw = _fox_bias_layout()
    const = lambda b, i: (0, 0)
    out_spec = pl.BlockSpec((None, tm, LANES), lambda b, i: (b, i, 0))
    qx, kx = pl.pallas_call(
        functools.partial(_fox_gate_kernel, tm=tm, inv_scale=inv_scale),
        grid=(bsz, seq // tm),
        in_specs=[pl.BlockSpec((None, tm, k), lambda b, i: (b, i, 0)),
                  pl.BlockSpec((k, LANES), const), pl.BlockSpec((1, LANES), const),
                  pl.BlockSpec((3 * LANES, LANES), const), pl.BlockSpec((3 * LANES, LANES), const),
                  pl.BlockSpec((1, LANES), const), pl.BlockSpec((1, LANES), const)],
        out_specs=[out_spec, out_spec],
        out_shape=[jax.ShapeDtypeStruct((bsz, seq, LANES), BF16)] * 2,
        scratch_shapes=[pltpu.VMEM((1, LANES), F32)],
        compiler_params=_params("parallel", "arbitrary"),
        name="fox_gate",
    )(x3, w_f, b_f, eq, ek, oq, ok)
    return qx, kx, gw


def _rms_rows(v, g):
    return v * lax.rsqrt(jnp.mean(v * v, axis=-1, keepdims=True) + RMS_EPS) * g


def _mla_down_kernel(x_ref, w_ref, gq_ref, gkv_ref, cos_ref, sa_ref, sb_ref, cq_ref, ckv_ref, kr_ref, *, half):
    acc = jnp.dot(x_ref[...], w_ref[...], preferred_element_type=F32)
    q_end, kv_end = MLA_Q_RANK, MLA_Q_RANK + MLA_KV_RANK
    cq_ref[...] = _rms_rows(acc[:, :q_end], gq_ref[...]).astype(BF16)
    ckv_ref[...] = _rms_rows(acc[:, q_end:kv_end], gkv_ref[...]).astype(BF16)
    kr_ref[...] = _apply_rot(acc[:, kv_end:], cos_ref[...], sa_ref[...], sb_ref[...], half).astype(BF16)


def _mla_down(x, w, gq, gkv, rot, *, tm=1024):
    m, k = x.shape
    n = w.shape[1]
    tm = min(tm, m)
    tabs, half = rot
    row = lambda i: (i, 0)
    const = lambda i: (0, 0)
    return pl.pallas_call(
        functools.partial(_mla_down_kernel, half=half),
        grid=(m // tm,),
        in_specs=[pl.BlockSpec((tm, k), row), pl.BlockSpec((k, n), const),
                  pl.BlockSpec((1, MLA_Q_RANK), const), pl.BlockSpec((1, MLA_KV_RANK), const)]
                 + [pl.BlockSpec((tm, LANES), row)] * 3,
        out_specs=[pl.BlockSpec((tm, MLA_Q_RANK), row), pl.BlockSpec((tm, MLA_KV_RANK), row),
                   pl.BlockSpec((tm, LANES), row)],
        out_shape=[jax.ShapeDtypeStruct((m, MLA_Q_RANK), BF16), jax.ShapeDtypeStruct((m, MLA_KV_RANK), BF16),
                   jax.ShapeDtypeStruct((m, LANES), BF16)],
        compiler_params=_params("parallel"),
        name="mla_down",
    )(x, w, gq.reshape(1, -1), gkv.reshape(1, -1), *tabs)


def _flash_kernel(*refs, tq, tk, mode, extra_gw, heads_per_block, moba, lam_init):
    it = iter(refs)
    q_ref, k_ref, vt_ref = next(it), next(it), next(it)
    qx_ref = kx_ref = km_ref = lamp_ref = subg_ref = g_scr = None
    if extra_gw is not None:
        qx_ref, kx_ref = next(it), next(it)
    if moba:
        km_ref = next(it)
    if mode == "diff":
        lamp_ref, subg_ref = next(it), next(it)
    o_ref = next(it)
    m_scr, acc_scr, s_scr, mx_scr = next(it), next(it), next(it), next(it)
    if moba:
        g_scr = next(it)

    hb = pl.program_id(1)
    qi = pl.program_id(2)
    lane = lax.broadcasted_iota(jnp.int32, (1, LANES), 1)
    q = q_ref[...]
    zero = jnp.zeros_like(q)
    q_half = [jnp.where(lane // HEAD_DIM == e, q, zero) for e in range(2)]

    if extra_gw is not None:
        groups = LANES // extra_gw
        qx = qx_ref[...]
        q_aug = []
        for e in range(2):
            grp = (hb * heads_per_block + e) % groups
            q_aug.append(jnp.concatenate([q_half[e], jnp.where(lane // extra_gw == grp, qx, jnp.zeros_like(qx))], -1))
    elif moba:
        q_aug = _moba_bias(q_half, km_ref, g_scr, qi, tq, moba)
    else:
        q_aug = q_half

    m_scr[...] = jnp.full_like(m_scr, NEG)
    acc_scr[...] = jnp.zeros_like(acc_scr)
    row = lax.broadcasted_iota(jnp.int32, (LANES, 1), 0)

    n_chunks = tq // QCHUNK

    def step(cur, nxt):
        if nxt is not None:
            n_slot, n_j, masked = nxt
            start = pl.multiple_of(n_j * tk, tk)
            k = k_ref[pl.ds(start, tk), :]
            if extra_gw is not None:
                k = jnp.concatenate([k, kx_ref[pl.ds(start, tk), :]], -1)
            elif moba:
                blk = (start + lax.broadcasted_iota(jnp.int32, (tk, 1), 0)) // MOBA_BLOCK
                k = jnp.concatenate([k, jnp.where(blk == lane, 1.0, 0.0).astype(BF16)], -1)
        if cur is not None:
            c_slot, c_j = cur
            vt = vt_ref[c_j]
            one = jnp.ones_like(vt)
            if mode == "pair":
                vt_e = [jnp.where(row // HEAD_DIM == e, vt, one) for e in range(2)]
            else:
                vt_e = [jnp.concatenate([vt, one[:ONES_ROWS]], 0)] * 2
        for e in range(2):
            for c in range(n_chunks):
                cols = slice(c * QCHUNK, (c + 1) * QCHUNK)
                if nxt is not None:
                    se = lax.dot_general(k, q_aug[e][cols, :], (((1,), (1,)), ((), ())),
                                         preferred_element_type=F32)
                    if masked:
                        key = start + lax.broadcasted_iota(jnp.int32, (tk, QCHUNK), 0)
                        qry = qi * tq + c * QCHUNK + lax.broadcasted_iota(jnp.int32, (tk, QCHUNK), 1)
                        se = jnp.where(key <= qry, se, NEG)
                    s_scr[n_slot, e, :, cols] = se
                    mx_scr[n_slot, e, :, cols] = jnp.max(se, axis=0, keepdims=True)
                if cur is not None:
                    m_old = m_scr[e, :, cols]
                    m_new = jnp.maximum(m_old, mx_scr[c_slot, e, :, cols])
                    p = jnp.exp2(s_scr[c_slot, e, :, cols] - m_new)
                    alpha = jnp.exp2(m_old - m_new)
                    acc_scr[e, :, cols] = alpha * acc_scr[e, :, cols] + jnp.dot(vt_e[e], p.astype(BF16),
                                                                               preferred_element_type=F32)
                    m_scr[e, :, cols] = m_new

    n_full = (qi * tq) // tk
    n_diag = max(tq // tk, 1)

    def drain(first, n_unmasked):
        total = n_unmasked + n_diag
        for i in range(total):
            nxt = ((i + 1) % 2, first + i + 1, i + 1 >= n_unmasked) if i + 1 < total else None
            step((i % 2, first + i), nxt)

    @pl.when(n_full == 0)
    def _():
        step(None, (0, 0, True))
        drain(0, 0)

    @pl.when(n_full > 0)
    def _():
        step(None, (0, 0, False))

        def body(t, carry):
            j = 2 * t
            step((0, j), (1, j + 1, False))
            step((1, j + 1), (0, j + 2, False))
            return carry

        n_pairs = (n_full - 1) // 2
        lax.fori_loop(0, n_pairs, body, 0)

        @pl.when(n_full % 2 == 1)
        def _():
            drain(2 * n_pairs, 1)

        @pl.when(n_full % 2 == 0)
        def _():
            drain(2 * n_pairs, 2)

    if mode == "pair":
        a0, a1 = acc_scr[0], acc_scr[1]
        o_t = jnp.concatenate([a0[:HEAD_DIM] / a0[HEAD_DIM:HEAD_DIM + 1], a1[HEAD_DIM:] / a1[0:1]], axis=0)
        o_ref[...] = o_t.T.astype(o_ref.dtype)
    else:
        o0 = (acc_scr[0, :LANES, :] / acc_scr[0, LANES:LANES + 1, :]).T
        o1 = (acc_scr[1, :LANES, :] / acc_scr[1, LANES:LANES + 1, :]).T
        lp = lamp_ref[...]
        lam = (jnp.exp(jnp.sum(lp[0:1] * lp[1:2], axis=-1, keepdims=True))
               - jnp.exp(jnp.sum(lp[2:3] * lp[3:4], axis=-1, keepdims=True)) + lam_init)
        o = o0 - lam * o1
        o = o * lax.rsqrt(jnp.mean(o * o, axis=-1, keepdims=True) + RMS_EPS) * subg_ref[...] * (1.0 - lam_init)
        o_ref[...] = o.astype(o_ref.dtype)


def _moba_bias(q_half, km_ref, g_scr, qi, tq, n_blocks):
    rows = max(8, -(-n_blocks // 8) * 8)
    km = km_ref[...]
    blk = lax.broadcasted_iota(jnp.int32, (LANES, 1), 0)
    cur = (qi * tq + lax.broadcasted_iota(jnp.int32, (1, tq), 1)) // MOBA_BLOCK
    for e in range(2):
        gate = lax.dot_general(km, q_half[e], (((1,), (1,)), ((), ())), preferred_element_type=F32)
        g_scr[e] = jnp.where(blk < cur, gate, NEG)
    blk_r = blk[:rows]

    def count(n, ranks):
        out = []
        for e in range(2):
            gn = g_scr[e, pl.ds(n, 1), :]
            g = g_scr[e, :rows, :]
            beats = (gn > g) | ((gn == g) & (n < blk_r))
            out.append(ranks[e] + jnp.where(beats, 1.0, 0.0))
        return tuple(out)

    n_past = (qi * tq + tq - 1) // MOBA_BLOCK
    ranks = lax.fori_loop(0, n_past, count, (jnp.zeros((rows, tq), F32),) * 2)
    q_aug = []
    for e in range(2):
        keep = ((blk_r < cur) & (ranks[e] < MOBA_TOPK)) | (blk_r == cur)
        bias_t = jnp.where(keep, 0.0, NEG)
        if rows < LANES:
            bias_t = jnp.concatenate([bias_t, jnp.zeros((LANES - rows, tq), F32)], axis=0)
        q_aug.append(jnp.concatenate([q_half[e], bias_t.T.astype(BF16)], -1))
    return q_aug


def _key_major(v, tk):
    bsz, seq, c = v.shape
    return v.reshape(bsz, seq // tk, tk, c).transpose(0, 1, 3, 2)


def _flash(q, k, vt, *, qcol, kcol, vcol, n_blocks, mode="pair", qx=None, qx_col=None, kx=None, extra_gw=None,
           km=None, lam_params=None, subg=None, lam_init=None, tq=1024, tk=1024):
    bsz, seq, _ = q.shape
    tq, tk = min(tq, seq), min(tk, seq)
    assert seq % tq == 0 and seq % tk == 0 and (tk % tq == 0 or tq % tk == 0) and tq % QCHUNK == 0
    heads_per_block = 2 if mode == "pair" else 1
    in_specs = [pl.BlockSpec((None, tq, LANES), lambda b, h, i: (b, i, qcol + h)),
                pl.BlockSpec((None, seq, LANES), lambda b, h, i: (b, 0, kcol + h)),
                pl.BlockSpec((None, seq // tk, LANES, tk), lambda b, h, i: (b, 0, vcol + h, 0))]
    args = [q, k, _key_major(vt, tk)]
    acc_rows = LANES if mode == "pair" else LANES + ONES_ROWS
    scratch = [pltpu.VMEM((2, 1, tq), F32), pltpu.VMEM((2, acc_rows, tq), F32), pltpu.VMEM((2, 2, tk, tq), F32),
               pltpu.VMEM((2, 2, 1, tq), F32)]
    if extra_gw is not None:
        in_specs += [pl.BlockSpec((None, tq, LANES), lambda b, h, i: (b, i, qx_col(h))),
                     pl.BlockSpec((None, seq, LANES), lambda b, h, i: (b, 0, 0))]
        args += [qx, kx]
    moba = 0
    if km is not None:
        moba = seq // MOBA_BLOCK
        assert moba <= LANES and seq % MOBA_BLOCK == 0 and min(MOBA_TOPK, max(moba - 1, 1)) == MOBA_TOPK
        in_specs.append(pl.BlockSpec((None, LANES, LANES), lambda b, h, i: (b, 0, h)))
        args.append(km)
        scratch.append(pltpu.VMEM((2, LANES, tq), F32))
    if mode == "diff":
        in_specs += [pl.BlockSpec((4, HEAD_DIM), lambda b, h, i: (0, 0)), pl.BlockSpec((1, LANES), lambda b, h, i: (0, 0))]
        args += [lam_params, subg]
    return pl.pallas_call(
        functools.partial(_flash_kernel, tq=tq, tk=tk, mode=mode, extra_gw=extra_gw,
                          heads_per_block=heads_per_block, moba=moba, lam_init=lam_init),
        grid=(bsz, n_blocks, seq // tq),
        in_specs=in_specs,
        out_specs=pl.BlockSpec((None, tq, LANES), lambda b, h, i: (b, i, h)),
        out_shape=jax.ShapeDtypeStruct((bsz, seq, n_blocks * LANES), BF16),
        scratch_shapes=scratch,
        compiler_params=_params("parallel", "parallel", "arbitrary"),
        name="flash_" + mode + ("_x" if extra_gw else "") + ("_moba" if moba else ""),
    )(*args)


def _diff_lambda_init(layer):
    return 0.8 - 0.6 * math.exp(-0.3 * layer)


def _diff_mixer(h16, bsz, seq, rot_p, w_qkv, lq1, lk1, lq2, lk2, subln_g, layer):
    width = DIFF_HEADS * 2 * HEAD_DIM
    wb = w_qkv.astype(BF16)
    qk = _proj(h16, wb[:, :2 * width], rot=rot_p, scale=HEAD_DIM ** -0.5 * LOG2E, scale_to=width)
    v = _proj(h16, wb[:, 2 * width:])
    nb = width // LANES
    lam_params = jnp.stack([lq1, lk1, lq2, lk2]).astype(F32)
    o = _flash(qk.reshape(bsz, seq, -1), qk.reshape(bsz, seq, -1), v.reshape(bsz, seq, -1),
               qcol=0, kcol=nb, vcol=0, n_blocks=nb, mode="diff",
               lam_params=lam_params, subg=subln_g.reshape(1, LANES).astype(F32), lam_init=_diff_lambda_init(layer))
    return o.reshape(bsz * seq, width)


def _fox_mixer(h16, bsz, seq, w_in, b_f):
    width = FOX_HEADS * HEAD_DIM
    scale = HEAD_DIM ** -0.5
    wb = w_in.astype(BF16)
    qkv = _proj(h16, wb[:, :3 * width], scale=scale * LOG2E, scale_to=width)
    w_f = jnp.pad(wb[:, 3 * width:], ((0, 0), (0, LANES - FOX_HEADS)))
    bf = jnp.pad(b_f.astype(F32), (0, LANES - FOX_HEADS)).reshape(1, LANES)
    qx, kx, gw = _fox_gate(h16.reshape(bsz, seq, -1), w_f, bf, inv_scale=LOG2E)
    nb = width // LANES
    qkv3 = qkv.reshape(bsz, seq, -1)
    o = _flash(qkv3, qkv3, qkv3[..., 2 * width:], qcol=0, kcol=nb, vcol=0, n_blocks=nb,
               qx=qx, qx_col=lambda h: 0, kx=kx, extra_gw=gw)
    return o.reshape(bsz * seq, width)


def _mla_mixer(h16, bsz, seq, rot_m, w_down, gq, gkv, w_uq, w_ukv):
    nope, rope, vd, heads = HEAD_DIM, MLA_ROPE, HEAD_DIM, MLA_HEADS
    wd = w_down.astype(BF16)
    kv_end = MLA_Q_RANK + MLA_KV_RANK
    wd = jnp.concatenate([wd[:, :kv_end]] + [wd[:, kv_end:]] * (LANES // rope), axis=1)
    cq, ckv, kr = _mla_down(h16, wd, gq.astype(F32), gkv.astype(F32), rot_m)
    wq = w_uq.astype(BF16).reshape(MLA_Q_RANK, heads, nope + rope)
    wq = jnp.concatenate([wq[:, :, :nope].reshape(MLA_Q_RANK, -1), wq[:, :, nope:].reshape(MLA_Q_RANK, -1)], axis=1)
    wkv = w_ukv.astype(BF16).reshape(MLA_KV_RANK, heads, nope + vd)
    wkv = jnp.concatenate([wkv[:, :, :nope].reshape(MLA_KV_RANK, -1), wkv[:, :, nope:].reshape(MLA_KV_RANK, -1)], axis=1)
    q = _proj(cq, wq, rot=rot_m, rot_from=heads * nope, scale=(nope + rope) ** -0.5 * LOG2E, scale_to=wq.shape[1])
    kv = _proj(ckv, wkv)
    nb = heads * nope // LANES
    groups = LANES // rope
    kv3 = kv.reshape(bsz, seq, -1)
    o = _flash(q.reshape(bsz, seq, -1), kv3, kv3[..., heads * nope:], qcol=0, kcol=0, vcol=0, n_blocks=nb,
               qx=q.reshape(bsz, seq, -1), qx_col=lambda h: nb + (2 * h) // groups, kx=kr.reshape(bsz, seq, LANES),
               extra_gw=rope)
    return o.reshape(bsz * seq, heads * vd)


def _moba_mixer(h16, bsz, seq, rot_p, w_qkv):
    width = MOBA_HEADS * HEAD_DIM
    wb = w_qkv.astype(BF16)
    qk, kmean = _proj(h16, wb[:, :2 * width], rot=rot_p, want_kmean=True, scale=HEAD_DIM ** -0.5 * LOG2E, scale_to=width)
    v = _proj(h16, wb[:, 2 * width:])
    n_kb = seq // MOBA_BLOCK
    km = kmean.reshape(bsz, n_kb, 2 * width)[:, :, width:].astype(BF16)
    km = jnp.pad(km, ((0, 0), (0, LANES - n_kb), (0, 0)))
    nb = width // LANES
    qk3 = qk.reshape(bsz, seq, -1)
    o = _flash(qk3, qk3, v.reshape(bsz, seq, -1), qcol=0, kcol=nb, vcol=0, n_blocks=nb, km=km)
    return o.reshape(bsz * seq, width)


def kernel(x, positions, diff_w_qkv, diff_lambda_q1, diff_lambda_k1, diff_lambda_q2, diff_lambda_k2, diff_subln_g,
           diff_w_o, fox_w_in, fox_b_f, fox_w_o, mla_w_down, mla_q_norm_g, mla_kv_norm_g, mla_w_uq, mla_w_ukv,
           mla_w_o, moba_w_qkv, moba_w_o, ffn_w_in, ffn_conv_w, ffn_conv_b, ffn_w_out, ln1_g, ln1_b, ln2_g, ln2_b):
    bsz, seq, d_model = x.shape
    depth = ffn_w_in.shape[0]
    alpha = (2 * depth) ** 0.25
    rot_p = _rot_tables(positions, PARTIAL_ROT, HEAD_DIM)
    rot_m = _rot_tables(positions, MLA_ROPE, MLA_ROPE)
    h32 = x.reshape(bsz * seq, d_model).astype(F32)
    h16 = h32.astype(BF16)
    for i in range(depth):
        m, u = i % N_MIXERS, i // N_MIXERS
        if m == 0:
            y = _diff_mixer(h16, bsz, seq, rot_p, diff_w_qkv[u], diff_lambda_q1[u], diff_lambda_k1[u],
                            diff_lambda_q2[u], diff_lambda_k2[u], diff_subln_g[u], i)
            w_o = diff_w_o[u]
        elif m == 1:
            y = _fox_mixer(h16, bsz, seq, fox_w_in[u], fox_b_f[u])
            w_o = fox_w_o[u]
        elif m == 2:
            y = _mla_mixer(h16, bsz, seq, rot_m, mla_w_down[u], mla_q_norm_g[u], mla_kv_norm_g[u], mla_w_uq[u],
                           mla_w_ukv[u])
            w_o = mla_w_o[u]
        else:
            y = _moba_mixer(h16, bsz, seq, rot_p, moba_w_qkv[u])
            w_o = moba_w_o[u]
        h32, h16 = _proj_ln(y, w_o.astype(BF16), h32, ln1_g[i].astype(F32), ln1_b[i].astype(F32), alpha=alpha)
        h32, h16 = _ffn(h16, ffn_w_in[i].astype(BF16), ffn_conv_w[i].astype(F32), ffn_conv_b[i].astype(F32),
                        ffn_w_out[i].astype(BF16), h32, ln2_g[i].astype(F32), ln2_b[i].astype(F32), alpha=alpha, seq=seq)
    return h32.reshape(bsz, seq, d_model).astype(x.dtype)
```

```python
import functools
import math

import jax
import jax.numpy as jnp
import numpy as np
from jax import lax
from jax.experimental import pallas as pl
from jax.experimental.pallas import tpu as pltpu

F32 = jnp.float32
BF16 = jnp.bfloat16

LANES = 128
VMEM_LIMIT = 56 * 1024 * 1024

ROPE_THETA = 500000.0
LN_EPS = 1e-5
RMS_EPS = 1e-6
HEAD_DIM = 64
PARTIAL_ROT = HEAD_DIM // 4
DIFF_HEADS = 8
FOX_HEADS = 16
MLA_HEADS = 16
MLA_ROPE = 32
MLA_Q_RANK = 384
MLA_KV_RANK = 256
MOBA_HEADS = 16
MOBA_BLOCK = 256
MOBA_TOPK = 3
CONV_WIDTH = 3
N_MIXERS = 4

QCHUNK = 256
ONES_ROWS = 8
NEG = -1e30
LOG2E = math.log2(math.e)


def _params(*sem):
    return pltpu.CompilerParams(dimension_semantics=sem, vmem_limit_bytes=VMEM_LIMIT)


def _rot_tables(positions, rot_dim, period):
    half = rot_dim // 2
    inv_freq = ROPE_THETA ** (-jnp.arange(0, rot_dim, 2, dtype=F32) / rot_dim)
    ang = positions.astype(F32)[..., None] * inv_freq
    cos, sin = jnp.cos(ang), jnp.sin(ang)
    lane = np.arange(LANES)
    j = lane % period
    idx = j % half
    lo = j < half
    hi = (j >= half) & (j < rot_dim)
    cos_t = jnp.where(lo | hi, cos[..., idx], 1.0)
    sin_a = jnp.where(lo, -sin[..., idx], 0.0)
    sin_b = jnp.where(hi, sin[..., idx], 0.0)
    rows = positions.shape[0] * positions.shape[1]
    return tuple(t.reshape(rows, LANES) for t in (cos_t, sin_a, sin_b)), half


def _apply_rot(val, cos, sin_a, sin_b, half):
    pieces = []
    for c in range(val.shape[1] // LANES):
        blk = val[:, c * LANES:(c + 1) * LANES]
        pieces.append(blk * cos + pltpu.roll(blk, LANES - half, 1) * sin_a + pltpu.roll(blk, half, 1) * sin_b)
    return pieces[0] if len(pieces) == 1 else jnp.concatenate(pieces, axis=-1)


def _proj_kernel(*refs, rot_half, rot_tile0, want_kmean, tm, scale, scale_tiles):
    x_ref, w_ref = refs[:2]
    rest = refs[2:]
    if rot_half is not None:
        cos_ref, sa_ref, sb_ref = rest[:3]
        rest = rest[3:]
    o_ref = rest[0]
    km_ref = rest[1] if want_kmean else None
    acc = jnp.dot(x_ref[...], w_ref[...], preferred_element_type=F32)

    def emit(val):
        if want_kmean:
            tn = val.shape[1]
            km_ref[...] = jnp.mean(val.reshape(tm // MOBA_BLOCK, MOBA_BLOCK, tn), axis=1)
        if scale is not None:
            val = val * jnp.where(pl.program_id(1) < scale_tiles, scale, 1.0)
        o_ref[...] = val.astype(o_ref.dtype)

    if rot_half is None:
        emit(acc)
        return

    def rotated():
        return _apply_rot(acc, cos_ref[...], sa_ref[...], sb_ref[...], rot_half)

    if rot_tile0 == 0:
        emit(rotated())
    else:
        j = pl.program_id(1)

        @pl.when(j >= rot_tile0)
        def _():
            emit(rotated())

        @pl.when(j < rot_tile0)
        def _():
            emit(acc)


def _proj(x, w, *, out_dtype=BF16, rot=None, rot_from=0, want_kmean=False, scale=None, scale_to=0, tm=1024, tn=512):
    m, k = x.shape
    n = w.shape[1]
    tm, tn = min(tm, m), min(tn, n)
    assert m % tm == 0 and n % tn == 0 and rot_from % tn == 0 and scale_to % tn == 0
    in_specs = [pl.BlockSpec((tm, k), lambda i, j: (i, 0)), pl.BlockSpec((k, tn), lambda i, j: (0, j))]
    args = [x, w]
    rot_half = None
    if rot is not None:
        tabs, rot_half = rot
        in_specs += [pl.BlockSpec((tm, LANES), lambda i, j: (i, 0))] * 3
        args += list(tabs)
    out_shape = [jax.ShapeDtypeStruct((m, n), out_dtype)]
    out_specs = [pl.BlockSpec((tm, tn), lambda i, j: (i, j))]
    if want_kmean:
        assert tm % MOBA_BLOCK == 0
        out_shape.append(jax.ShapeDtypeStruct((m // tm, tm // MOBA_BLOCK, n), F32))
        out_specs.append(pl.BlockSpec((None, tm // MOBA_BLOCK, tn), lambda i, j: (i, 0, j)))
    res = pl.pallas_call(
        functools.partial(_proj_kernel, rot_half=rot_half, rot_tile0=rot_from // tn, want_kmean=want_kmean, tm=tm,
                          scale=scale, scale_tiles=scale_to // tn),
        grid=(m // tm, n // tn),
        in_specs=in_specs,
        out_specs=out_specs,
        out_shape=out_shape,
        compiler_params=_params("parallel", "arbitrary"),
        name="proj",
    )(*args)
    return res if want_kmean else res[0]


def _layer_norm_rows(z, g, b):
    mu = jnp.mean(z, axis=-1, keepdims=True)
    zc = z - mu
    var = jnp.mean(zc * zc, axis=-1, keepdims=True)
    return zc * lax.rsqrt(var + LN_EPS) * g + b


def _proj_ln_kernel(x_ref, w_ref, res_ref, g_ref, b_ref, h32_ref, h16_ref, *, alpha):
    y = jnp.dot(x_ref[...], w_ref[...], preferred_element_type=F32)
    h = _layer_norm_rows(alpha * res_ref[...] + y, g_ref[...], b_ref[...])
    h32_ref[...] = h
    h16_ref[...] = h.astype(BF16)


def _proj_ln(x, w, res, g, b, *, alpha, tm=512):
    m, k = x.shape
    n = w.shape[1]
    tm = min(tm, m)
    row = lambda i: (i, 0)
    const = lambda i: (0, 0)
    return pl.pallas_call(
        functools.partial(_proj_ln_kernel, alpha=alpha),
        grid=(m // tm,),
        in_specs=[pl.BlockSpec((tm, k), row), pl.BlockSpec((k, n), const), pl.BlockSpec((tm, n), row),
                  pl.BlockSpec((1, n), const), pl.BlockSpec((1, n), const)],
        out_specs=[pl.BlockSpec((tm, n), row), pl.BlockSpec((tm, n), row)],
        out_shape=[jax.ShapeDtypeStruct((m, n), F32), jax.ShapeDtypeStruct((m, n), BF16)],
        compiler_params=_params("parallel"),
        name="proj_ln",
    )(x, w, res, g.reshape(1, n), b.reshape(1, n))


HALO = 16


def _ffn_kernel(x_ref, halo_ref, win_ref, cw_ref, cb_ref, wout_ref, res_ref, g_ref, b_ref, h32_ref, h16_ref,
                *, alpha, tm, seq, chunk):
    i = pl.program_id(0)
    d_ff = wout_ref.shape[0]
    x = x_ref[...]
    halo = halo_ref[...]
    halo = jnp.where((i * tm) % seq == 0, jnp.zeros_like(halo), halo)
    xa = jnp.concatenate([halo, x], axis=0)
    def up_proj(c):
        cs = slice(c * chunk, (c + 1) * chunk)
        gate = jnp.dot(xa, win_ref[:, cs], preferred_element_type=F32)
        up = jnp.dot(x, win_ref[:, d_ff + c * chunk:d_ff + (c + 1) * chunk], preferred_element_type=F32)
        return gate, up

    acc = jnp.zeros((tm, wout_ref.shape[1]), F32)
    n_c = d_ff // chunk
    nxt = up_proj(0)
    for c in range(n_c):
        cs = slice(c * chunk, (c + 1) * chunk)
        gate, up = nxt
        if c + 1 < n_c:
            nxt = up_proj(c + 1)
        g0 = gate[HALO:]
        g1 = pltpu.roll(gate, 1, 0)[HALO:]
        g2 = pltpu.roll(gate, 2, 0)[HALO:]
        cw = cw_ref[:, cs]
        conv = cw[0:1, :] * g2 + cw[1:2, :] * g1 + cw[2:3, :] * g0 + cb_ref[:, cs]
        act = conv * (1.0 / (1.0 + jnp.exp(-conv))) * up
        acc = acc + jnp.dot(act.astype(BF16), wout_ref[cs, :], preferred_element_type=F32)
    h = _layer_norm_rows(alpha * res_ref[...] + acc, g_ref[...], b_ref[...])
    h32_ref[...] = h
    h16_ref[...] = h.astype(BF16)


def _ffn(x16, w_in, conv_w, conv_b, w_out, res, g, b, *, alpha, seq, tm=512, chunk=256):
    m, d = x16.shape
    d_ff, n = w_out.shape
    tm = min(tm, m)
    assert seq % tm == 0 and tm % HALO == 0 and d_ff % chunk == 0 and w_in.shape == (d, 2 * d_ff)
    row = lambda i: (i, 0)
    const = lambda i: (0, 0)
    once = pl.Buffered(1)
    sub = tm // HALO
    return pl.pallas_call(
        functools.partial(_ffn_kernel, alpha=alpha, tm=tm, seq=seq, chunk=chunk),
        grid=(m // tm,),
        in_specs=[pl.BlockSpec((tm, d), row),
                  pl.BlockSpec((HALO, d), lambda i: (jnp.maximum(i * sub - 1, 0), 0)),
                  pl.BlockSpec((d, 2 * d_ff), const, pipeline_mode=once),
                  pl.BlockSpec((CONV_WIDTH, d_ff), const), pl.BlockSpec((1, d_ff), const),
                  pl.BlockSpec((d_ff, n), const, pipeline_mode=once), pl.BlockSpec((tm, n), row),
                  pl.BlockSpec((1, n), const), pl.BlockSpec((1, n), const)],
        out_specs=[pl.BlockSpec((tm, n), row), pl.BlockSpec((tm, n), row)],
        out_shape=[jax.ShapeDtypeStruct((m, n), F32), jax.ShapeDtypeStruct((m, n), BF16)],
        compiler_params=_params("parallel"),
        name="ffn",
    )(x16, x16, w_in, conv_w, conv_b.reshape(1, d_ff), w_out, res, g.reshape(1, n), b.reshape(1, n))


def _split3_bf16(v):
    hi = v.astype(BF16)
    r1 = v - hi.astype(F32)
    mid = r1.astype(BF16)
    lo = (r1 - mid.astype(F32)).astype(BF16)
    return hi, mid, lo


def _fox_gate_kernel(x_ref, w_ref, bf_ref, eq_ref, ek_ref, oq_ref, ok_ref, qx_ref, kx_ref, carry_ref, *, tm, inv_scale):
    @pl.when(pl.program_id(1) == 0)
    def _():
        carry_ref[...] = jnp.zeros_like(carry_ref)

    z = jnp.dot(x_ref[...], w_ref[...], preferred_element_type=F32) + bf_ref[...]
    log_f = jnp.minimum(z, 0.0) - jnp.log(1.0 + jnp.exp(-jnp.abs(z)))
    r = lax.broadcasted_iota(jnp.int32, (tm, tm), 0)
    c = lax.broadcasted_iota(jnp.int32, (tm, tm), 1)
    tri = (c <= r).astype(F32)
    csum = jnp.dot(tri, log_f, preferred_element_type=F32, precision=lax.Precision.HIGHEST) + carry_ref[...]
    carry_ref[...] = csum[tm - 1:tm, :]
    parts = jnp.concatenate(_split3_bf16(csum * inv_scale), axis=-1)
    qx_ref[...] = (jnp.dot(parts, eq_ref[...], preferred_element_type=F32) + oq_ref[...]).astype(BF16)
    kx_ref[...] = (jnp.dot(parts, ek_ref[...], preferred_element_type=F32) + ok_ref[...]).astype(BF16)


def _fox_bias_layout():
    gw = LANES // FOX_HEADS
    eq = np.zeros((3 * LANES, LANES), np.float32)
    ek = np.zeros((3 * LANES, LANES), np.float32)
    oq = np.zeros((1, LANES), np.float32)
    ok = np.zeros((1, LANES), np.float32)
    for h in range(FOX_HEADS):
        for t in range(3):
            eq[t * LANES + h, gw * h + t] = 1.0
            ek[t * LANES + h, gw * h + 3 + t] = -1.0
            oq[0, gw * h + 3 + t] = 1.0
            ok[0, gw * h + t] = 1.0
    return jnp.asarray(eq, BF16), jnp.asarray(ek, BF16), jnp.asarray(oq), jnp.asarray(ok), gw


def _fox_gate(x3, w_f, b_f, *, inv_scale, tm=256):
    bsz, seq, k = x3.shape
    tm = min(tm, seq)
    eq, ek, oq, ok, gw = _fox_bias_layout()
    const = lambda b, i: (0, 0)
    out_spec = pl.BlockSpec((None, tm, LANES), lambda b, i: (b, i, 0))
    qx, kx = pl.pallas_call(
        functools.partial(_fox_gate_kernel, tm=tm, inv_scale=inv_scale),
        grid=(bsz, seq // tm),
        in_specs=[pl.BlockSpec((None, tm, k), lambda b, i: (b, i, 0)),
                  pl.BlockSpec((k, LANES), const), pl.BlockSpec((1, LANES), const),
                  pl.BlockSpec((3 * LANES, LANES), const), pl.BlockSpec((3 * LANES, LANES), const),
                  pl.BlockSpec((1, LANES), const), pl.BlockSpec((1, LANES), const)],
        out_specs=[out_spec, out_spec],
        out_shape=[jax.ShapeDtypeStruct((bsz, seq, LANES), BF16)] * 2,
        scratch_shapes=[pltpu.VMEM((1, LANES), F32)],
        compiler_params=_params("parallel", "arbitrary"),
        name="fox_gate",
    )(x3, w_f, b_f, eq, ek, oq, ok)
    return qx, kx, gw


def _rms_rows(v, g):
    return v * lax.rsqrt(jnp.mean(v * v, axis=-1, keepdims=True) + RMS_EPS) * g


def _mla_down_kernel(x_ref, w_ref, gq_ref, gkv_ref, cos_ref, sa_ref, sb_ref, cq_ref, ckv_ref, kr_ref, *, half):
    acc = jnp.dot(x_ref[...], w_ref[...], preferred_element_type=F32)
    q_end, kv_end = MLA_Q_RANK, MLA_Q_RANK + MLA_KV_RANK
    cq_ref[...] = _rms_rows(acc[:, :q_end], gq_ref[...]).astype(BF16)
    ckv_ref[...] = _rms_rows(acc[:, q_end:kv_end], gkv_ref[...]).astype(BF16)
    kr_ref[...] = _apply_rot(acc[:, kv_end:], cos_ref[...], sa_ref[...], sb_ref[...], half).astype(BF16)


def _mla_down(x, w, gq, gkv, rot, *, tm=1024):
    m, k = x.shape
    n = w.shape[1]
    tm = min(tm, m)
    tabs, half = rot
    row = lambda i: (i, 0)
    const = lambda i: (0, 0)
    return pl.pallas_call(
        functools.partial(_mla_down_kernel, half=half),
        grid=(m // tm,),
        in_specs=[pl.BlockSpec((tm, k), row), pl.BlockSpec((k, n), const),
                  pl.BlockSpec((1, MLA_Q_RANK), const), pl.BlockSpec((1, MLA_KV_RANK), const)]
                 + [pl.BlockSpec((tm, LANES), row)] * 3,
        out_specs=[pl.BlockSpec((tm, MLA_Q_RANK), row), pl.BlockSpec((tm, MLA_KV_RANK), row),
                   pl.BlockSpec((tm, LANES), row)],
        out_shape=[jax.ShapeDtypeStruct((m, MLA_Q_RANK), BF16), jax.ShapeDtypeStruct((m, MLA_KV_RANK), BF16),
                   jax.ShapeDtypeStruct((m, LANES), BF16)],
        compiler_params=_params("parallel"),
        name="mla_down",
    )(x, w, gq.reshape(1, -1), gkv.reshape(1, -1), *tabs)


def _flash_kernel(*refs, tq, tk, mode, extra_gw, heads_per_block, moba, lam_init):
    it = iter(refs)
    q_ref, k_ref, vt_ref = next(it), next(it), next(it)
    qx_ref = kx_ref = km_ref = kb_ref = lamp_ref = subg_ref = g_scr = None
    if extra_gw is not None:
        qx_ref, kx_ref = next(it), next(it)
    if moba:
        km_ref, kb_ref = next(it), next(it)
    if mode == "diff":
        lamp_ref, subg_ref = next(it), next(it)
    o_ref = next(it)
    m_scr, acc_scr, s_scr, mx_scr = next(it), next(it), next(it), next(it)
    if moba:
        g_scr = next(it)

    hb = pl.program_id(1)
    qi = pl.program_id(2)
    lane = lax.broadcasted_iota(jnp.int32, (1, LANES), 1)
    q = q_ref[...]
    zero = jnp.zeros_like(q)
    q_half = [jnp.where(lane // HEAD_DIM == e, q, zero) for e in range(2)]

    if extra_gw is not None:
        groups = LANES // extra_gw
        qx = qx_ref[...]
        q_aug = []
        for e in range(2):
            grp = (hb * heads_per_block + e) % groups
            q_aug.append(jnp.concatenate([q_half[e], jnp.where(lane // extra_gw == grp, qx, jnp.zeros_like(qx))], -1))
    elif moba:
        q_aug = _moba_bias(q_half, km_ref, g_scr, qi, tq, moba)
    else:
        q_aug = q_half

    m_scr[...] = jnp.full_like(m_scr, NEG)
    acc_scr[...] = jnp.zeros_like(acc_scr)
    row = lax.broadcasted_iota(jnp.int32, (LANES, 1), 0)

    n_chunks = tq // QCHUNK

    def step(par, cur, nxt, masked):
        cur_slot = (1 - par, par)
        nxt_slot = (par, 1 - par)
        k, start, vt_e = [None, None], [None, None], [None, None]
        for e in range(2):
            if nxt[e] is not None:
                start[e] = pl.multiple_of(nxt[e] * tk, tk)
                k[e] = k_ref[pl.ds(start[e], tk), :]
                if extra_gw is not None:
                    k[e] = jnp.concatenate([k[e], kx_ref[pl.ds(start[e], tk), :]], -1)
                elif moba:
                    k[e] = jnp.concatenate([k[e], kb_ref[pl.ds(start[e], tk), :]], -1)
            if cur[e] is not None:
                vt = vt_ref[cur[e]]
                one = jnp.ones_like(vt)
                if mode == "pair":
                    vt_e[e] = jnp.where(row // HEAD_DIM == e, vt, one)
                else:
                    vt_e[e] = jnp.concatenate([vt, one[:ONES_ROWS]], 0)
        for c in range(n_chunks):
            cols = slice(c * QCHUNK, (c + 1) * QCHUNK)
            for e in range(2):
                if nxt[e] is not None:
                    se = lax.dot_general(k[e], q_aug[e][cols, :], (((1,), (1,)), ((), ())),
                                         preferred_element_type=F32)
                    if masked[e]:
                        key = start[e] + lax.broadcasted_iota(jnp.int32, (tk, QCHUNK), 0)
                        qry = qi * tq + c * QCHUNK + lax.broadcasted_iota(jnp.int32, (tk, QCHUNK), 1)
                        se = jnp.where(key <= qry, se, NEG)
                    s_scr[nxt_slot[e], e, :, cols] = se
                    mx_scr[nxt_slot[e], e, :, cols] = jnp.max(se, axis=0, keepdims=True)
                if cur[e] is not None:
                    m_old = m_scr[e, :, cols]
                    m_new = jnp.maximum(m_old, mx_scr[cur_slot[e], e, :, cols])
                    p = jnp.exp2(s_scr[cur_slot[e], e, :, cols] - m_new)
                    alpha = jnp.exp2(m_old - m_new)
                    acc_scr[e, :, cols] = alpha * acc_scr[e, :, cols] + jnp.dot(vt_e[e], p.astype(BF16),
                                                                               preferred_element_type=F32)
                    m_scr[e, :, cols] = m_new

    n_full = qi
    no = None

    @pl.when(n_full == 0)
    def _():
        step(0, (no, no), (0, no), (True, False))
        step(1, (0, no), (no, 0), (False, True))
        step(0, (no, 0), (no, no), (False, False))

    @pl.when(n_full == 1)
    def _():
        step(0, (no, no), (0, no), (False, False))
        step(1, (0, no), (1, 0), (True, False))
        step(0, (1, 0), (no, 1), (False, True))
        step(1, (no, 1), (no, no), (False, False))

    @pl.when(n_full >= 2)
    def _():
        step(0, (no, no), (0, no), (False, False))
        step(1, (0, no), (1, 0), (False, False))

        def body(t, carry):
            i = 2 + 2 * t
            step(0, (i - 1, i - 2), (i, i - 1), (False, False))
            step(1, (i, i - 1), (i + 1, i), (False, False))
            return carry

        lax.fori_loop(0, (n_full - 2) // 2, body, 0)
        last = n_full

        @pl.when(n_full % 2 == 0)
        def _():
            step(0, (last - 1, last - 2), (last, last - 1), (True, False))
            step(1, (last, last - 1), (no, last), (False, True))
            step(0, (no, last), (no, no), (False, False))

        @pl.when(n_full % 2 == 1)
        def _():
            step(0, (last - 2, last - 3), (last - 1, last - 2), (False, False))
            step(1, (last - 1, last - 2), (last, last - 1), (True, False))
            step(0, (last, last - 1), (no, last), (False, True))
            step(1, (no, last), (no, no), (False, False))

    if mode == "pair":
        a0, a1 = acc_scr[0], acc_scr[1]
        o_t = jnp.concatenate([a0[:HEAD_DIM] / a0[HEAD_DIM:HEAD_DIM + 1], a1[HEAD_DIM:] / a1[0:1]], axis=0)
        o_ref[...] = o_t.T.astype(o_ref.dtype)
    else:
        o0 = (acc_scr[0, :LANES, :] / acc_scr[0, LANES:LANES + 1, :]).T
        o1 = (acc_scr[1, :LANES, :] / acc_scr[1, LANES:LANES + 1, :]).T
        lp = lamp_ref[...]
        lam = (jnp.exp(jnp.sum(lp[0:1] * lp[1:2], axis=-1, keepdims=True))
               - jnp.exp(jnp.sum(lp[2:3] * lp[3:4], axis=-1, keepdims=True)) + lam_init)
        o = o0 - lam * o1
        o = o * lax.rsqrt(jnp.mean(o * o, axis=-1, keepdims=True) + RMS_EPS) * subg_ref[...] * (1.0 - lam_init)
        o_ref[...] = o.astype(o_ref.dtype)


def _moba_bias(q_half, km_ref, g_scr, qi, tq, n_blocks):
    rows = max(8, -(-n_blocks // 8) * 8)
    km = km_ref[...]
    blk = lax.broadcasted_iota(jnp.int32, (LANES, 1), 0)
    cur = (qi * tq + lax.broadcasted_iota(jnp.int32, (1, tq), 1)) // MOBA_BLOCK
    for e in range(2):
        gate = lax.dot_general(km, q_half[e], (((1,), (1,)), ((), ())), preferred_element_type=F32)
        g_scr[e] = jnp.where(blk < cur, gate, NEG)
    blk_r = blk[:rows]

    def count(n, ranks):
        out = []
        for e in range(2):
            gn = g_scr[e, pl.ds(n, 1), :]
            g = g_scr[e, :rows, :]
            beats = (gn > g) | ((gn == g) & (n < blk_r))
            out.append(ranks[e] + jnp.where(beats, 1.0, 0.0))
        return tuple(out)

    n_past = (qi * tq + tq - 1) // MOBA_BLOCK
    ranks = lax.fori_loop(0, n_past, count, (jnp.zeros((rows, tq), F32),) * 2)
    q_aug = []
    for e in range(2):
        keep = ((blk_r < cur) & (ranks[e] < MOBA_TOPK)) | (blk_r == cur)
        bias_t = jnp.where(keep, 0.0, NEG)
        if rows < LANES:
            bias_t = jnp.concatenate([bias_t, jnp.zeros((LANES - rows, tq), F32)], axis=0)
        q_aug.append(jnp.concatenate([q_half[e], bias_t.T.astype(BF16)], -1))
    return q_aug


def _key_major(v, tk):
    bsz, seq, c = v.shape
    return v.reshape(bsz, seq // tk, tk, c).transpose(0, 1, 3, 2)


def _flash(q, k, vt, *, qcol, kcol, vcol, n_blocks, mode="pair", qx=None, qx_col=None, kx=None, extra_gw=None,
           km=None, lam_params=None, subg=None, lam_init=None, tq=1024, tk=1024):
    bsz, seq, _ = q.shape
    tq, tk = min(tq, seq), min(tk, seq)
    assert seq % tq == 0 and tq == tk and tq % QCHUNK == 0
    heads_per_block = 2 if mode == "pair" else 1
    in_specs = [pl.BlockSpec((None, tq, LANES), lambda b, h, i: (b, i, qcol + h)),
                pl.BlockSpec((None, seq, LANES), lambda b, h, i: (b, 0, kcol + h)),
                pl.BlockSpec((None, seq // tk, LANES, tk), lambda b, h, i: (b, 0, vcol + h, 0))]
    args = [q, k, _key_major(vt, tk)]
    acc_rows = LANES if mode == "pair" else LANES + ONES_ROWS
    scratch = [pltpu.VMEM((2, 1, tq), F32), pltpu.VMEM((2, acc_rows, tq), F32), pltpu.VMEM((2, 2, tk, tq), F32),
               pltpu.VMEM((2, 2, 1, tq), F32)]
    if extra_gw is not None:
        in_specs += [pl.BlockSpec((None, tq, LANES), lambda b, h, i: (b, i, qx_col(h))),
                     pl.BlockSpec((None, seq, LANES), lambda b, h, i: (b, 0, 0))]
        args += [qx, kx]
    moba = 0
    if km is not None:
        moba = seq // MOBA_BLOCK
        assert moba <= LANES and seq % MOBA_BLOCK == 0 and min(MOBA_TOPK, max(moba - 1, 1)) == MOBA_TOPK
        key_block = (np.arange(seq)[:, None] // MOBA_BLOCK == np.arange(LANES)[None, :])
        in_specs += [pl.BlockSpec((None, LANES, LANES), lambda b, h, i: (b, 0, h)),
                     pl.BlockSpec((seq, LANES), lambda b, h, i: (0, 0))]
        args += [km, jnp.asarray(key_block, BF16)]
        scratch.append(pltpu.VMEM((2, LANES, tq), F32))
    if mode == "diff":
        in_specs += [pl.BlockSpec((4, HEAD_DIM), lambda b, h, i: (0, 0)), pl.BlockSpec((1, LANES), lambda b, h, i: (0, 0))]
        args += [lam_params, subg]
    return pl.pallas_call(
        functools.partial(_flash_kernel, tq=tq, tk=tk, mode=mode, extra_gw=extra_gw,
                          heads_per_block=heads_per_block, moba=moba, lam_init=lam_init),
        grid=(bsz, n_blocks, seq // tq),
        in_specs=in_specs,
        out_specs=pl.BlockSpec((None, tq, LANES), lambda b, h, i: (b, i, h)),
        out_shape=jax.ShapeDtypeStruct((bsz, seq, n_blocks * LANES), BF16),
        scratch_shapes=scratch,
        compiler_params=_params("parallel", "parallel", "arbitrary"),
        name="flash_" + mode + ("_x" if extra_gw else "") + ("_moba" if moba else ""),
    )(*args)


def _diff_lambda_init(layer):
    return 0.8 - 0.6 * math.exp(-0.3 * layer)


def _diff_mixer(h16, bsz, seq, rot_p, w_qkv, lq1, lk1, lq2, lk2, subln_g, layer):
    width = DIFF_HEADS * 2 * HEAD_DIM
    wb = w_qkv.astype(BF16)
    qk = _proj(h16, wb[:, :2 * width], rot=rot_p, scale=HEAD_DIM ** -0.5 * LOG2E, scale_to=width)
    v = _proj(h16, wb[:, 2 * width:])
    nb = width // LANES
    lam_params = jnp.stack([lq1, lk1, lq2, lk2]).astype(F32)
    o = _flash(qk.reshape(bsz, seq, -1), qk.reshape(bsz, seq, -1), v.reshape(bsz, seq, -1),
               qcol=0, kcol=nb, vcol=0, n_blocks=nb, mode="diff",
               lam_params=lam_params, subg=subln_g.reshape(1, LANES).astype(F32), lam_init=_diff_lambda_init(layer))
    return o.reshape(bsz * seq, width)


def _fox_mixer(h16, bsz, seq, w_in, b_f):
    width = FOX_HEADS * HEAD_DIM
    scale = HEAD_DIM ** -0.5
    wb = w_in.astype(BF16)
    qkv = _proj(h16, wb[:, :3 * width], scale=scale * LOG2E, scale_to=width)
    w_f = jnp.pad(wb[:, 3 * width:], ((0, 0), (0, LANES - FOX_HEADS)))
    bf = jnp.pad(b_f.astype(F32), (0, LANES - FOX_HEADS)).reshape(1, LANES)
    qx, kx, gw = _fox_gate(h16.reshape(bsz, seq, -1), w_f, bf, inv_scale=LOG2E)
    nb = width // LANES
    qkv3 = qkv.reshape(bsz, seq, -1)
    o = _flash(qkv3, qkv3, qkv3[..., 2 * width:], qcol=0, kcol=nb, vcol=0, n_blocks=nb,
               qx=qx, qx_col=lambda h: 0, kx=kx, extra_gw=gw)
    return o.reshape(bsz * seq, width)


def _mla_mixer(h16, bsz, seq, rot_m, w_down, gq, gkv, w_uq, w_ukv):
    nope, rope, vd, heads = HEAD_DIM, MLA_ROPE, HEAD_DIM, MLA_HEADS
    wd = w_down.astype(BF16)
    kv_end = MLA_Q_RANK + MLA_KV_RANK
    wd = jnp.concatenate([wd[:, :kv_end]] + [wd[:, kv_end:]] * (LANES // rope), axis=1)
    cq, ckv, kr = _mla_down(h16, wd, gq.astype(F32), gkv.astype(F32), rot_m)
    wq = w_uq.astype(BF16).reshape(MLA_Q_RANK, heads, nope + rope)
    wq = jnp.concatenate([wq[:, :, :nope].reshape(MLA_Q_RANK, -1), wq[:, :, nope:].reshape(MLA_Q_RANK, -1)], axis=1)
    wkv = w_ukv.astype(BF16).reshape(MLA_KV_RANK, heads, nope + vd)
    wkv = jnp.concatenate([wkv[:, :, :nope].reshape(MLA_KV_RANK, -1), wkv[:, :, nope:].reshape(MLA_KV_RANK, -1)], axis=1)
    q = _proj(cq, wq, rot=rot_m, rot_from=heads * nope, scale=(nope + rope) ** -0.5 * LOG2E, scale_to=wq.shape[1])
    kv = _proj(ckv, wkv)
    nb = heads * nope // LANES
    groups = LANES // rope
    kv3 = kv.reshape(bsz, seq, -1)
    o = _flash(q.reshape(bsz, seq, -1), kv3, kv3[..., heads * nope:], qcol=0, kcol=0, vcol=0, n_blocks=nb,
               qx=q.reshape(bsz, seq, -1), qx_col=lambda h: nb + (2 * h) // groups, kx=kr.reshape(bsz, seq, LANES),
               extra_gw=rope)
    return o.reshape(bsz * seq, heads * vd)


def _moba_mixer(h16, bsz, seq, rot_p, w_qkv):
    width = MOBA_HEADS * HEAD_DIM
    wb = w_qkv.astype(BF16)
    qk, kmean = _proj(h16, wb[:, :2 * width], rot=rot_p, want_kmean=True, scale=HEAD_DIM ** -0.5 * LOG2E, scale_to=width)
    v = _proj(h16, wb[:, 2 * width:])
    n_kb = seq // MOBA_BLOCK
    km = kmean.reshape(bsz, n_kb, 2 * width)[:, :, width:].astype(BF16)
    km = jnp.pad(km, ((0, 0), (0, LANES - n_kb), (0, 0)))
    nb = width // LANES
    qk3 = qk.reshape(bsz, seq, -1)
    o = _flash(qk3, qk3, v.reshape(bsz, seq, -1), qcol=0, kcol=nb, vcol=0, n_blocks=nb, km=km)
    return o.reshape(bsz * seq, width)


def kernel(x, positions, diff_w_qkv, diff_lambda_q1, diff_lambda_k1, diff_lambda_q2, diff_lambda_k2, diff_subln_g,
           diff_w_o, fox_w_in, fox_b_f, fox_w_o, mla_w_down, mla_q_norm_g, mla_kv_norm_g, mla_w_uq, mla_w_ukv,
           mla_w_o, moba_w_qkv, moba_w_o, ffn_w_in, ffn_conv_w, ffn_conv_b, ffn_w_out, ln1_g, ln1_b, ln2_g, ln2_b):
    bsz, seq, d_model = x.shape
    depth = ffn_w_in.shape[0]
    alpha = (2 * depth) ** 0.25
    rot_p = _rot_tables(positions, PARTIAL_ROT, HEAD_DIM)
    rot_m = _rot_tables(positions, MLA_ROPE, MLA_ROPE)
    h32 = x.reshape(bsz * seq, d_model).astype(F32)
    h16 = h32.astype(BF16)
    for i in range(depth):
        m, u = i % N_MIXERS, i // N_MIXERS
        if m == 0:
            y = _diff_mixer(h16, bsz, seq, rot_p, diff_w_qkv[u], diff_lambda_q1[u], diff_lambda_k1[u],
                            diff_lambda_q2[u], diff_lambda_k2[u], diff_subln_g[u], i)
            w_o = diff_w_o[u]
        elif m == 1:
            y = _fox_mixer(h16, bsz, seq, fox_w_in[u], fox_b_f[u])
            w_o = fox_w_o[u]
        elif m == 2:
            y = _mla_mixer(h16, bsz, seq, rot_m, mla_w_down[u], mla_q_norm_g[u], mla_kv_norm_g[u], mla_w_uq[u],
                           mla_w_ukv[u])
            w_o = mla_w_o[u]
        else:
            y = _moba_mixer(h16, bsz, seq, rot_p, moba_w_qkv[u])
            w_o = moba_w_o[u]
        h32, h16 = _proj_ln(y, w_o.astype(BF16), h32, ln1_g[i].astype(F32), ln1_b[i].astype(F32), alpha=alpha)
        h32, h16 = _ffn(h16, ffn_w_in[i].astype(BF16), ffn_conv_w[i].astype(F32), ffn_conv_b[i].astype(F32),
                        ffn_w_out[i].astype(BF16), h32, ln2_g[i].astype(F32), ln2_b[i].astype(F32), alpha=alpha, seq=seq)
    return h32.reshape(bsz, seq, d_model).astype(x.dtype)
```

```python
import functools
import math

import jax
import jax.numpy as jnp
import numpy as np
from jax import lax
from jax.experimental import pallas as pl
from jax.experimental.pallas import tpu as pltpu

F32 = jnp.float32
BF16 = jnp.bfloat16

LANES = 128
VMEM_LIMIT = 56 * 1024 * 1024

ROPE_THETA = 500000.0
LN_EPS = 1e-5
RMS_EPS = 1e-6
HEAD_DIM = 64
PARTIAL_ROT = HEAD_DIM // 4
DIFF_HEADS = 8
FOX_HEADS = 16
MLA_HEADS = 16
MLA_ROPE = 32
MLA_Q_RANK = 384
MLA_KV_RANK = 256
MOBA_HEADS = 16
MOBA_BLOCK = 256
MOBA_TOPK = 3
CONV_WIDTH = 3
N_MIXERS = 4

QCHUNK = 256
ONES_ROWS = 8
NEG = -1e30
LOG2E = math.log2(math.e)


def _params(*sem):
    return pltpu.CompilerParams(dimension_semantics=sem, vmem_limit_bytes=VMEM_LIMIT)


def _rot_tables(positions, rot_dim, period):
    half = rot_dim // 2
    inv_freq = ROPE_THETA ** (-jnp.arange(0, rot_dim, 2, dtype=F32) / rot_dim)
    ang = positions.astype(F32)[..., None] * inv_freq
    cos, sin = jnp.cos(ang), jnp.sin(ang)
    lane = np.arange(LANES)
    j = lane % period
    idx = j % half
    lo = j < half
    hi = (j >= half) & (j < rot_dim)
    cos_t = jnp.where(lo | hi, cos[..., idx], 1.0)
    sin_a = jnp.where(lo, -sin[..., idx], 0.0)
    sin_b = jnp.where(hi, sin[..., idx], 0.0)
    rows = positions.shape[0] * positions.shape[1]
    return tuple(t.reshape(rows, LANES) for t in (cos_t, sin_a, sin_b)), half


def _apply_rot(val, cos, sin_a, sin_b, half):
    pieces = []
    for c in range(val.shape[1] // LANES):
        blk = val[:, c * LANES:(c + 1) * LANES]
        pieces.append(blk * cos + pltpu.roll(blk, LANES - half, 1) * sin_a + pltpu.roll(blk, half, 1) * sin_b)
    return pieces[0] if len(pieces) == 1 else jnp.concatenate(pieces, axis=-1)


def _proj_kernel(*refs, rot_half, rot_tile0, want_kmean, tm, scale, scale_tiles):
    x_ref, w_ref = refs[:2]
    rest = refs[2:]
    if rot_half is not None:
        cos_ref, sa_ref, sb_ref = rest[:3]
        rest = rest[3:]
    o_ref = rest[0]
    km_ref = rest[1] if want_kmean else None
    acc = jnp.dot(x_ref[...], w_ref[...], preferred_element_type=F32)

    def emit(val):
        if want_kmean:
            tn = val.shape[1]
            km_ref[...] = jnp.mean(val.reshape(tm // MOBA_BLOCK, MOBA_BLOCK, tn), axis=1)
        if scale is not None:
            val = val * jnp.where(pl.program_id(1) < scale_tiles, scale, 1.0)
        o_ref[...] = val.astype(o_ref.dtype)

    if rot_half is None:
        emit(acc)
        return

    def rotated():
        return _apply_rot(acc, cos_ref[...], sa_ref[...], sb_ref[...], rot_half)

    if rot_tile0 == 0:
        emit(rotated())
    else:
        j = pl.program_id(1)

        @pl.when(j >= rot_tile0)
        def _():
            emit(rotated())

        @pl.when(j < rot_tile0)
        def _():
            emit(acc)


def _proj(x, w, *, out_dtype=BF16, rot=None, rot_from=0, want_kmean=False, scale=None, scale_to=0, tm=2048, tn=512):
    m, k = x.shape
    n = w.shape[1]
    tm, tn = min(tm, m), min(tn, n)
    assert m % tm == 0 and n % tn == 0 and rot_from % tn == 0 and scale_to % tn == 0
    in_specs = [pl.BlockSpec((tm, k), lambda i, j: (i, 0)), pl.BlockSpec((k, tn), lambda i, j: (0, j))]
    args = [x, w]
    rot_half = None
    if rot is not None:
        tabs, rot_half = rot
        in_specs += [pl.BlockSpec((tm, LANES), lambda i, j: (i, 0))] * 3
        args += list(tabs)
    out_shape = [jax.ShapeDtypeStruct((m, n), out_dtype)]
    out_specs = [pl.BlockSpec((tm, tn), lambda i, j: (i, j))]
    if want_kmean:
        assert tm % MOBA_BLOCK == 0
        out_shape.append(jax.ShapeDtypeStruct((m // tm, tm // MOBA_BLOCK, n), F32))
        out_specs.append(pl.BlockSpec((None, tm // MOBA_BLOCK, tn), lambda i, j: (i, 0, j)))
    res = pl.pallas_call(
        functools.partial(_proj_kernel, rot_half=rot_half, rot_tile0=rot_from // tn, want_kmean=want_kmean, tm=tm,
                          scale=scale, scale_tiles=scale_to // tn),
        grid=(m // tm, n // tn),
        in_specs=in_specs,
        out_specs=out_specs,
        out_shape=out_shape,
        compiler_params=_params("parallel", "arbitrary"),
        name="proj",
    )(*args)
    return res if want_kmean else res[0]


def _layer_norm_rows(z, g, b):
    mu = jnp.mean(z, axis=-1, keepdims=True)
    zc = z - mu
    var = jnp.mean(zc * zc, axis=-1, keepdims=True)
    return zc * lax.rsqrt(var + LN_EPS) * g + b


def _proj_ln_kernel(x_ref, w_ref, res_ref, g_ref, b_ref, h32_ref, h16_ref, *, alpha):
    y = jnp.dot(x_ref[...], w_ref[...], preferred_element_type=F32)
    h = _layer_norm_rows(alpha * res_ref[...] + y, g_ref[...], b_ref[...])
    h32_ref[...] = h
    h16_ref[...] = h.astype(BF16)


def _proj_ln(x, w, res, g, b, *, alpha, tm=512):
    m, k = x.shape
    n = w.shape[1]
    tm = min(tm, m)
    row = lambda i: (i, 0)
    const = lambda i: (0, 0)
    return pl.pallas_call(
        functools.partial(_proj_ln_kernel, alpha=alpha),
        grid=(m // tm,),
        in_specs=[pl.BlockSpec((tm, k), row), pl.BlockSpec((k, n), const), pl.BlockSpec((tm, n), row),
                  pl.BlockSpec((1, n), const), pl.BlockSpec((1, n), const)],
        out_specs=[pl.BlockSpec((tm, n), row), pl.BlockSpec((tm, n), row)],
        out_shape=[jax.ShapeDtypeStruct((m, n), F32), jax.ShapeDtypeStruct((m, n), BF16)],
        compiler_params=_params("parallel"),
        name="proj_ln",
    )(x, w, res, g.reshape(1, n), b.reshape(1, n))


HALO = 16


def _ffn_kernel(x_ref, halo_ref, win_ref, cw_ref, cb_ref, wout_ref, res_ref, g_ref, b_ref, h32_ref, h16_ref,
                *, alpha, tm, seq, chunk):
    i = pl.program_id(0)
    d_ff = wout_ref.shape[0]
    x = x_ref[...]
    halo = halo_ref[...]
    halo = jnp.where((i * tm) % seq == 0, jnp.zeros_like(halo), halo)
    xa = jnp.concatenate([halo, x], axis=0)
    def up_proj(c):
        cs = slice(c * chunk, (c + 1) * chunk)
        gate = jnp.dot(xa, win_ref[:, cs], preferred_element_type=F32)
        up = jnp.dot(x, win_ref[:, d_ff + c * chunk:d_ff + (c + 1) * chunk], preferred_element_type=F32)
        return gate, up

    acc = jnp.zeros((tm, wout_ref.shape[1]), F32)
    n_c = d_ff // chunk
    nxt = up_proj(0)
    for c in range(n_c):
        cs = slice(c * chunk, (c + 1) * chunk)
        gate, up = nxt
        if c + 1 < n_c:
            nxt = up_proj(c + 1)
        g0 = gate[HALO:]
        g1 = pltpu.roll(gate, 1, 0)[HALO:]
        g2 = pltpu.roll(gate, 2, 0)[HALO:]
        cw = cw_ref[:, cs]
        conv = cw[0:1, :] * g2 + cw[1:2, :] * g1 + cw[2:3, :] * g0 + cb_ref[:, cs]
        act = conv * (1.0 / (1.0 + jnp.exp(-conv))) * up
        acc = acc + jnp.dot(act.astype(BF16), wout_ref[cs, :], preferred_element_type=F32)
    h = _layer_norm_rows(alpha * res_ref[...] + acc, g_ref[...], b_ref[...])
    h32_ref[...] = h
    h16_ref[...] = h.astype(BF16)


def _ffn(x16, w_in, conv_w, conv_b, w_out, res, g, b, *, alpha, seq, tm=512, chunk=256):
    m, d = x16.shape
    d_ff, n = w_out.shape
    tm = min(tm, m)
    assert seq % tm == 0 and tm % HALO == 0 and d_ff % chunk == 0 and w_in.shape == (d, 2 * d_ff)
    row = lambda i: (i, 0)
    const = lambda i: (0, 0)
    once = pl.Buffered(1)
    sub = tm // HALO
    return pl.pallas_call(
        functools.partial(_ffn_kernel, alpha=alpha, tm=tm, seq=seq, chunk=chunk),
        grid=(m // tm,),
        in_specs=[pl.BlockSpec((tm, d), row),
                  pl.BlockSpec((HALO, d), lambda i: (jnp.maximum(i * sub - 1, 0), 0)),
                  pl.BlockSpec((d, 2 * d_ff), const, pipeline_mode=once),
                  pl.BlockSpec((CONV_WIDTH, d_ff), const), pl.BlockSpec((1, d_ff), const),
                  pl.BlockSpec((d_ff, n), const, pipeline_mode=once), pl.BlockSpec((tm, n), row),
                  pl.BlockSpec((1, n), const), pl.BlockSpec((1, n), const)],
        out_specs=[pl.BlockSpec((tm, n), row), pl.BlockSpec((tm, n), row)],
        out_shape=[jax.ShapeDtypeStruct((m, n), F32), jax.ShapeDtypeStruct((m, n), BF16)],
        compiler_params=_params("parallel"),
        name="ffn",
    )(x16, x16, w_in, conv_w, conv_b.reshape(1, d_ff), w_out, res, g.reshape(1, n), b.reshape(1, n))


def _split3_bf16(v):
    hi = v.astype(BF16)
    r1 = v - hi.astype(F32)
    mid = r1.astype(BF16)
    lo = (r1 - mid.astype(F32)).astype(BF16)
    return hi, mid, lo


def _fox_gate_kernel(x_ref, w_ref, bf_ref, eq_ref, ek_ref, oq_ref, ok_ref, qx_ref, kx_ref, carry_ref, *, tm, inv_scale):
    @pl.when(pl.program_id(1) == 0)
    def _():
        carry_ref[...] = jnp.zeros_like(carry_ref)

    z = jnp.dot(x_ref[...], w_ref[...], preferred_element_type=F32) + bf_ref[...]
    log_f = jnp.minimum(z, 0.0) - jnp.log(1.0 + jnp.exp(-jnp.abs(z)))
    r = lax.broadcasted_iota(jnp.int32, (tm, tm), 0)
    c = lax.broadcasted_iota(jnp.int32, (tm, tm), 1)
    tri = (c <= r).astype(F32)
    csum = jnp.dot(tri, log_f, preferred_element_type=F32, precision=lax.Precision.HIGHEST) + carry_ref[...]
    carry_ref[...] = csum[tm - 1:tm, :]
    parts = jnp.concatenate(_split3_bf16(csum * inv_scale), axis=-1)
    qx_ref[...] = (jnp.dot(parts, eq_ref[...], preferred_element_type=F32) + oq_ref[...]).astype(BF16)
    kx_ref[...] = (jnp.dot(parts, ek_ref[...], preferred_element_type=F32) + ok_ref[...]).astype(BF16)


def _fox_bias_layout():
    gw = LANES // FOX_HEADS
    eq = np.zeros((3 * LANES, LANES), np.float32)
    ek = np.zeros((3 * LANES, LANES), np.float32)
    oq = np.zeros((1, LANES), np.float32)
    ok = np.zeros((1, LANES), np.float32)
    for h in range(FOX_HEADS):
        for t in range(3):
            eq[t * LANES + h, gw * h + t] = 1.0
            ek[t * LANES + h, gw * h + 3 + t] = -1.0
            oq[0, gw * h + 3 + t] = 1.0
            ok[0, gw * h + t] = 1.0
    return jnp.asarray(eq, BF16), jnp.asarray(ek, BF16), jnp.asarray(oq), jnp.asarray(ok), gw


def _fox_gate(x3, w_f, b_f, *, inv_scale, tm=256):
    bsz, seq, k = x3.shape
    tm = min(tm, seq)
    eq, ek, oq, ok, gw = _fox_bias_layout()
    const = lambda b, i: (0, 0)
    out_spec = pl.BlockSpec((None, tm, LANES), lambda b, i: (b, i, 0))
    qx, kx = pl.pallas_call(
        functools.partial(_fox_gate_kernel, tm=tm, inv_scale=inv_scale),
        grid=(bsz, seq // tm),
        in_specs=[pl.BlockSpec((None, tm, k), lambda b, i: (b, i, 0)),
                  pl.BlockSpec((k, LANES), const), pl.BlockSpec((1, LANES), const),
                  pl.BlockSpec((3 * LANES, LANES), const), pl.BlockSpec((3 * LANES, LANES), const),
                  pl.BlockSpec((1, LANES), const), pl.BlockSpec((1, LANES), const)],
        out_specs=[out_spec, out_spec],
        out_shape=[jax.ShapeDtypeStruct((bsz, seq, LANES), BF16)] * 2,
        scratch_shapes=[pltpu.VMEM((1, LANES), F32)],
        compiler_params=_params("parallel", "arbitrary"),
        name="fox_gate",
    )(x3, w_f, b_f, eq, ek, oq, ok)
    return qx, kx, gw


def _rms_rows(v, g):
    return v * lax.rsqrt(jnp.mean(v * v, axis=-1, keepdims=True) + RMS_EPS) * g


def _mla_down_kernel(x_ref, w_ref, gq_ref, gkv_ref, cos_ref, sa_ref, sb_ref, cq_ref, ckv_ref, kr_ref, *, half):
    acc = jnp.dot(x_ref[...], w_ref[...], preferred_element_type=F32)
    q_end, kv_end = MLA_Q_RANK, MLA_Q_RANK + MLA_KV_RANK
    cq_ref[...] = _rms_rows(acc[:, :q_end], gq_ref[...]).astype(BF16)
    ckv_ref[...] = _rms_rows(acc[:, q_end:kv_end], gkv_ref[...]).astype(BF16)
    kr_ref[...] = _apply_rot(acc[:, kv_end:], cos_ref[...], sa_ref[...], sb_ref[...], half).astype(BF16)


def _mla_down(x, w, gq, gkv, rot, *, tm=1024):
    m, k = x.shape
    n = w.shape[1]
    tm = min(tm, m)
    tabs, half = rot
    row = lambda i: (i, 0)
    const = lambda i: (0, 0)
    return pl.pallas_call(
        functools.partial(_mla_down_kernel, half=half),
        grid=(m // tm,),
        in_specs=[pl.BlockSpec((tm, k), row), pl.BlockSpec((k, n), const),
                  pl.BlockSpec((1, MLA_Q_RANK), const), pl.BlockSpec((1, MLA_KV_RANK), const)]
                 + [pl.BlockSpec((tm, LANES), row)] * 3,
        out_specs=[pl.BlockSpec((tm, MLA_Q_RANK), row), pl.BlockSpec((tm, MLA_KV_RANK), row),
                   pl.BlockSpec((tm, LANES), row)],
        out_shape=[jax.ShapeDtypeStruct((m, MLA_Q_RANK), BF16), jax.ShapeDtypeStruct((m, MLA_KV_RANK), BF16),
                   jax.ShapeDtypeStruct((m, LANES), BF16)],
        compiler_params=_params("parallel"),
        name="mla_down",
    )(x, w, gq.reshape(1, -1), gkv.reshape(1, -1), *tabs)


def _flash_kernel(*refs, tq, tk, mode, extra_gw, heads_per_block, moba, lam_init):
    it = iter(refs)
    q_ref, k_ref, vt_ref = next(it), next(it), next(it)
    qx_ref = kx_ref = km_ref = kb_ref = lamp_ref = subg_ref = g_scr = None
    if extra_gw is not None:
        qx_ref, kx_ref = next(it), next(it)
    if moba:
        km_ref, kb_ref = next(it), next(it)
    if mode == "diff":
        lamp_ref, subg_ref = next(it), next(it)
    o_ref = next(it)
    m_scr, acc_scr, s_scr, mx_scr = next(it), next(it), next(it), next(it)
    if moba:
        g_scr = next(it)

    hb = pl.program_id(1)
    qi = pl.program_id(2)
    lane = lax.broadcasted_iota(jnp.int32, (1, LANES), 1)
    q = q_ref[...]
    zero = jnp.zeros_like(q)
    q_half = [jnp.where(lane // HEAD_DIM == e, q, zero) for e in range(2)]

    if extra_gw is not None:
        groups = LANES // extra_gw
        qx = qx_ref[...]
        q_aug = []
        for e in range(2):
            grp = (hb * heads_per_block + e) % groups
            q_aug.append(jnp.concatenate([q_half[e], jnp.where(lane // extra_gw == grp, qx, jnp.zeros_like(qx))], -1))
    elif moba:
        q_aug = _moba_bias(q_half, km_ref, g_scr, qi, tq, moba)
    else:
        q_aug = q_half

    m_scr[...] = jnp.full_like(m_scr, NEG)
    acc_scr[...] = jnp.zeros_like(acc_scr)
    row = lax.broadcasted_iota(jnp.int32, (LANES, 1), 0)

    n_chunks = tq // QCHUNK

    def step(par, cur, nxt, masked):
        cur_slot = (1 - par, par)
        nxt_slot = (par, 1 - par)
        k, start, vt_e = [None, None], [None, None], [None, None]
        for e in range(2):
            if nxt[e] is not None:
                start[e] = pl.multiple_of(nxt[e] * tk, tk)
                k[e] = k_ref[pl.ds(start[e], tk), :]
                if extra_gw is not None:
                    k[e] = jnp.concatenate([k[e], kx_ref[pl.ds(start[e], tk), :]], -1)
                elif moba:
                    k[e] = jnp.concatenate([k[e], kb_ref[pl.ds(start[e], tk), :]], -1)
            if cur[e] is not None:
                vt = vt_ref[cur[e]]
                one = jnp.ones_like(vt)
                if mode == "pair":
                    vt_e[e] = jnp.where(row // HEAD_DIM == e, vt, one)
                else:
                    vt_e[e] = jnp.concatenate([vt, one[:ONES_ROWS]], 0)
        for c in range(n_chunks):
            cols = slice(c * QCHUNK, (c + 1) * QCHUNK)
            for e in range(2):
                if nxt[e] is not None:
                    se = lax.dot_general(k[e], q_aug[e][cols, :], (((1,), (1,)), ((), ())),
                                         preferred_element_type=F32)
                    if masked[e]:
                        key = start[e] + lax.broadcasted_iota(jnp.int32, (tk, QCHUNK), 0)
                        qry = qi * tq + c * QCHUNK + lax.broadcasted_iota(jnp.int32, (tk, QCHUNK), 1)
                        se = jnp.where(key <= qry, se, NEG)
                    s_scr[nxt_slot[e], e, :, cols] = se
                    mx_scr[nxt_slot[e], e, :, cols] = jnp.max(se, axis=0, keepdims=True)
                if cur[e] is not None:
                    m_old = m_scr[e, :, cols]
                    m_new = jnp.maximum(m_old, mx_scr[cur_slot[e], e, :, cols])
                    p = jnp.exp2(s_scr[cur_slot[e], e, :, cols] - m_new)
                    alpha = jnp.exp2(m_old - m_new)
                    acc_scr[e, :, cols] = alpha * acc_scr[e, :, cols] + jnp.dot(vt_e[e], p.astype(BF16),
                                                                               preferred_element_type=F32)
                    m_scr[e, :, cols] = m_new

    n_full = qi
    no = None

    @pl.when(n_full == 0)
    def _():
        step(0, (no, no), (0, no), (True, False))
        step(1, (0, no), (no, 0), (False, True))
        step(0, (no, 0), (no, no), (False, False))

    @pl.when(n_full == 1)
    def _():
        step(0, (no, no), (0, no), (False, False))
        step(1, (0, no), (1, 0), (True, False))
        step(0, (1, 0), (no, 1), (False, True))
        step(1, (no, 1), (no, no), (False, False))

    @pl.when(n_full >= 2)
    def _():
        step(0, (no, no), (0, no), (False, False))
        step(1, (0, no), (1, 0), (False, False))

        def body(t, carry):
            i = 2 + 2 * t
            step(0, (i - 1, i - 2), (i, i - 1), (False, False))
            step(1, (i, i - 1), (i + 1, i), (False, False))
            return carry

        lax.fori_loop(0, (n_full - 2) // 2, body, 0)
        last = n_full

        @pl.when(n_full % 2 == 0)
        def _():
            step(0, (last - 1, last - 2), (last, last - 1), (True, False))
            step(1, (last, last - 1), (no, last), (False, True))
            step(0, (no, last), (no, no), (False, False))

        @pl.when(n_full % 2 == 1)
        def _():
            step(0, (last - 2, last - 3), (last - 1, last - 2), (False, False))
            step(1, (last - 1, last - 2), (last, last - 1), (True, False))
            step(0, (last, last - 1), (no, last), (False, True))
            step(1, (no, last), (no, no), (False, False))

    if mode == "pair":
        a0, a1 = acc_scr[0], acc_scr[1]
        o_t = jnp.concatenate([a0[:HEAD_DIM] / a0[HEAD_DIM:HEAD_DIM + 1], a1[HEAD_DIM:] / a1[0:1]], axis=0)
        o_ref[...] = o_t.T.astype(o_ref.dtype)
    else:
        o0 = (acc_scr[0, :LANES, :] / acc_scr[0, LANES:LANES + 1, :]).T
        o1 = (acc_scr[1, :LANES, :] / acc_scr[1, LANES:LANES + 1, :]).T
        lp = lamp_ref[...]
        lam = (jnp.exp(jnp.sum(lp[0:1] * lp[1:2], axis=-1, keepdims=True))
               - jnp.exp(jnp.sum(lp[2:3] * lp[3:4], axis=-1, keepdims=True)) + lam_init)
        o = o0 - lam * o1
        o = o * lax.rsqrt(jnp.mean(o * o, axis=-1, keepdims=True) + RMS_EPS) * subg_ref[...] * (1.0 - lam_init)
        o_ref[...] = o.astype(o_ref.dtype)


def _moba_bias(q_half, km_ref, g_scr, qi, tq, n_blocks):
    rows = max(8, -(-n_blocks // 8) * 8)
    km = km_ref[...]
    blk = lax.broadcasted_iota(jnp.int32, (LANES, 1), 0)
    cur = (qi * tq + lax.broadcasted_iota(jnp.int32, (1, tq), 1)) // MOBA_BLOCK
    for e in range(2):
        gate = lax.dot_general(km, q_half[e], (((1,), (1,)), ((), ())), preferred_element_type=F32)
        g_scr[e] = jnp.where(blk < cur, gate, NEG)
    blk_r = blk[:rows]

    def count(n, ranks):
        out = []
        for e in range(2):
            gn = g_scr[e, pl.ds(n, 1), :]
            g = g_scr[e, :rows, :]
            beats = (gn > g) | ((gn == g) & (n < blk_r))
            out.append(ranks[e] + jnp.where(beats, 1.0, 0.0))
        return tuple(out)

    n_past = (qi * tq + tq - 1) // MOBA_BLOCK
    ranks = lax.fori_loop(0, n_past, count, (jnp.zeros((rows, tq), F32),) * 2)
    q_aug = []
    for e in range(2):
        keep = ((blk_r < cur) & (ranks[e] < MOBA_TOPK)) | (blk_r == cur)
        bias_t = jnp.where(keep, 0.0, NEG)
        if rows < LANES:
            bias_t = jnp.concatenate([bias_t, jnp.zeros((LANES - rows, tq), F32)], axis=0)
        q_aug.append(jnp.concatenate([q_half[e], bias_t.T.astype(BF16)], -1))
    return q_aug


def _key_major(v, tk):
    bsz, seq, c = v.shape
    return v.reshape(bsz, seq // tk, tk, c).transpose(0, 1, 3, 2)


def _flash(q, k, vt, *, qcol, kcol, vcol, n_blocks, mode="pair", qx=None, qx_col=None, kx=None, extra_gw=None,
           km=None, lam_params=None, subg=None, lam_init=None, tq=1024, tk=1024):
    bsz, seq, _ = q.shape
    tq, tk = min(tq, seq), min(tk, seq)
    assert seq % tq == 0 and tq == tk and tq % QCHUNK == 0
    heads_per_block = 2 if mode == "pair" else 1
    in_specs = [pl.BlockSpec((None, tq, LANES), lambda b, h, i: (b, i, qcol + h)),
                pl.BlockSpec((None, seq, LANES), lambda b, h, i: (b, 0, kcol + h)),
                pl.BlockSpec((None, seq // tk, LANES, tk), lambda b, h, i: (b, 0, vcol + h, 0))]
    args = [q, k, _key_major(vt, tk)]
    acc_rows = LANES if mode == "pair" else LANES + ONES_ROWS
    scratch = [pltpu.VMEM((2, 1, tq), F32), pltpu.VMEM((2, acc_rows, tq), F32), pltpu.VMEM((2, 2, tk, tq), F32),
               pltpu.VMEM((2, 2, 1, tq), F32)]
    if extra_gw is not None:
        in_specs += [pl.BlockSpec((None, tq, LANES), lambda b, h, i: (b, i, qx_col(h))),
                     pl.BlockSpec((None, seq, LANES), lambda b, h, i: (b, 0, 0))]
        args += [qx, kx]
    moba = 0
    if km is not None:
        moba = seq // MOBA_BLOCK
        assert moba <= LANES and seq % MOBA_BLOCK == 0 and min(MOBA_TOPK, max(moba - 1, 1)) == MOBA_TOPK
        key_block = (np.arange(seq)[:, None] // MOBA_BLOCK == np.arange(LANES)[None, :])
        in_specs += [pl.BlockSpec((None, LANES, LANES), lambda b, h, i: (b, 0, h)),
                     pl.BlockSpec((seq, LANES), lambda b, h, i: (0, 0))]
        args += [km, jnp.asarray(key_block, BF16)]
        scratch.append(pltpu.VMEM((2, LANES, tq), F32))
    if mode == "diff":
        in_specs += [pl.BlockSpec((4, HEAD_DIM), lambda b, h, i: (0, 0)), pl.BlockSpec((1, LANES), lambda b, h, i: (0, 0))]
        args += [lam_params, subg]
    return pl.pallas_call(
        functools.partial(_flash_kernel, tq=tq, tk=tk, mode=mode, extra_gw=extra_gw,
                          heads_per_block=heads_per_block, moba=moba, lam_init=lam_init),
        grid=(bsz, n_blocks, seq // tq),
        in_specs=in_specs,
        out_specs=pl.BlockSpec((None, tq, LANES), lambda b, h, i: (b, i, h)),
        out_shape=jax.ShapeDtypeStruct((bsz, seq, n_blocks * LANES), BF16),
        scratch_shapes=scratch,
        compiler_params=_params("parallel", "parallel", "arbitrary"),
        name="flash_" + mode + ("_x" if extra_gw else "") + ("_moba" if moba else ""),
    )(*args)


def _diff_lambda_init(layer):
    return 0.8 - 0.6 * math.exp(-0.3 * layer)


def _diff_mixer(h16, bsz, seq, rot_p, w_qkv, lq1, lk1, lq2, lk2, subln_g, layer):
    width = DIFF_HEADS * 2 * HEAD_DIM
    wb = w_qkv.astype(BF16)
    qk = _proj(h16, wb[:, :2 * width], rot=rot_p, scale=HEAD_DIM ** -0.5 * LOG2E, scale_to=width)
    v = _proj(h16, wb[:, 2 * width:])
    nb = width // LANES
    lam_params = jnp.stack([lq1, lk1, lq2, lk2]).astype(F32)
    o = _flash(qk.reshape(bsz, seq, -1), qk.reshape(bsz, seq, -1), v.reshape(bsz, seq, -1),
               qcol=0, kcol=nb, vcol=0, n_blocks=nb, mode="diff",
               lam_params=lam_params, subg=subln_g.reshape(1, LANES).astype(F32), lam_init=_diff_lambda_init(layer))
    return o.reshape(bsz * seq, width)


def _fox_mixer(h16, bsz, seq, w_in, b_f):
    width = FOX_HEADS * HEAD_DIM
    scale = HEAD_DIM ** -0.5
    wb = w_in.astype(BF16)
    qkv = _proj(h16, wb[:, :3 * width], scale=scale * LOG2E, scale_to=width)
    w_f = jnp.pad(wb[:, 3 * width:], ((0, 0), (0, LANES - FOX_HEADS)))
    bf = jnp.pad(b_f.astype(F32), (0, LANES - FOX_HEADS)).reshape(1, LANES)
    qx, kx, gw = _fox_gate(h16.reshape(bsz, seq, -1), w_f, bf, inv_scale=LOG2E)
    nb = width // LANES
    qkv3 = qkv.reshape(bsz, seq, -1)
    o = _flash(qkv3, qkv3, qkv3[..., 2 * width:], qcol=0, kcol=nb, vcol=0, n_blocks=nb,
               qx=qx, qx_col=lambda h: 0, kx=kx, extra_gw=gw)
    return o.reshape(bsz * seq, width)


def _mla_mixer(h16, bsz, seq, rot_m, w_down, gq, gkv, w_uq, w_ukv):
    nope, rope, vd, heads = HEAD_DIM, MLA_ROPE, HEAD_DIM, MLA_HEADS
    wd = w_down.astype(BF16)
    kv_end = MLA_Q_RANK + MLA_KV_RANK
    wd = jnp.concatenate([wd[:, :kv_end]] + [wd[:, kv_end:]] * (LANES // rope), axis=1)
    cq, ckv, kr = _mla_down(h16, wd, gq.astype(F32), gkv.astype(F32), rot_m)
    wq = w_uq.astype(BF16).reshape(MLA_Q_RANK, heads, nope + rope)
    wq = jnp.concatenate([wq[:, :, :nope].reshape(MLA_Q_RANK, -1), wq[:, :, nope:].reshape(MLA_Q_RANK, -1)], axis=1)
    wkv = w_ukv.astype(BF16).reshape(MLA_KV_RANK, heads, nope + vd)
    wkv = jnp.concatenate([wkv[:, :, :nope].reshape(MLA_KV_RANK, -1), wkv[:, :, nope:].reshape(MLA_KV_RANK, -1)], axis=1)
    q = _proj(cq, wq, rot=rot_m, rot_from=heads * nope, scale=(nope + rope) ** -0.5 * LOG2E, scale_to=wq.shape[1])
    kv = _proj(ckv, wkv)
    nb = heads * nope // LANES
    groups = LANES // rope
    kv3 = kv.reshape(bsz, seq, -1)
    o = _flash(q.reshape(bsz, seq, -1), kv3, kv3[..., heads * nope:], qcol=0, kcol=0, vcol=0, n_blocks=nb,
               qx=q.reshape(bsz, seq, -1), qx_col=lambda h: nb + (2 * h) // groups, kx=kr.reshape(bsz, seq, LANES),
               extra_gw=rope)
    return o.reshape(bsz * seq, heads * vd)


def _moba_mixer(h16, bsz, seq, rot_p, w_qkv):
    width = MOBA_HEADS * HEAD_DIM
    wb = w_qkv.astype(BF16)
    qk, kmean = _proj(h16, wb[:, :2 * width], rot=rot_p, want_kmean=True, scale=HEAD_DIM ** -0.5 * LOG2E, scale_to=width)
    v = _proj(h16, wb[:, 2 * width:])
    n_kb = seq // MOBA_BLOCK
    km = kmean.reshape(bsz, n_kb, 2 * width)[:, :, width:].astype(BF16)
    km = jnp.pad(km, ((0, 0), (0, LANES - n_kb), (0, 0)))
    nb = width // LANES
    qk3 = qk.reshape(bsz, seq, -1)
    o = _flash(qk3, qk3, v.reshape(bsz, seq, -1), qcol=0, kcol=nb, vcol=0, n_blocks=nb, km=km)
    return o.reshape(bsz * seq, width)


def kernel(x, positions, diff_w_qkv, diff_lambda_q1, diff_lambda_k1, diff_lambda_q2, diff_lambda_k2, diff_subln_g,
           diff_w_o, fox_w_in, fox_b_f, fox_w_o, mla_w_down, mla_q_norm_g, mla_kv_norm_g, mla_w_uq, mla_w_ukv,
           mla_w_o, moba_w_qkv, moba_w_o, ffn_w_in, ffn_conv_w, ffn_conv_b, ffn_w_out, ln1_g, ln1_b, ln2_g, ln2_b):
    bsz, seq, d_model = x.shape
    depth = ffn_w_in.shape[0]
    alpha = (2 * depth) ** 0.25
    rot_p = _rot_tables(positions, PARTIAL_ROT, HEAD_DIM)
    rot_m = _rot_tables(positions, MLA_ROPE, MLA_ROPE)
    h32 = x.reshape(bsz * seq, d_model).astype(F32)
    h16 = h32.astype(BF16)
    for i in range(depth):
        m, u = i % N_MIXERS, i // N_MIXERS
        if m == 0:
            y = _diff_mixer(h16, bsz, seq, rot_p, diff_w_qkv[u], diff_lambda_q1[u], diff_lambda_k1[u],
                            diff_lambda_q2[u], diff_lambda_k2[u], diff_subln_g[u], i)
            w_o = diff_w_o[u]
        elif m == 1:
            y = _fox_mixer(h16, bsz, seq, fox_w_in[u], fox_b_f[u])
            w_o = fox_w_o[u]
        elif m == 2:
            y = _mla_mixer(h16, bsz, seq, rot_m, mla_w_down[u], mla_q_norm_g[u], mla_kv_norm_g[u], mla_w_uq[u],
                           mla_w_ukv[u])
            w_o = mla_w_o[u]
        else:
            y = _moba_mixer(h16, bsz, seq, rot_p, moba_w_qkv[u])
            w_o = moba_w_o[u]
        h32, h16 = _proj_ln(y, w_o.astype(BF16), h32, ln1_g[i].astype(F32), ln1_b[i].astype(F32), alpha=alpha)
        h32, h16 = _ffn(h16, ffn_w_in[i].astype(BF16), ffn_conv_w[i].astype(F32), ffn_conv_b[i].astype(F32),
                        ffn_w_out[i].astype(BF16), h32, ln2_g[i].astype(F32), ln2_b[i].astype(F32), alpha=alpha, seq=seq)
    return h32.reshape(bsz, seq, d_model).astype(x.dtype)
```
